```python
import math
import jax, jax.numpy as jnp
from jax import lax
import numpy as np

D_MODEL = 1024
BATCH = 8
SEQ = 2048
DEPTH = 2

SSD_INNER = 1024
SSD_HEADS = 16
SSD_HEAD_DIM = 64
SSD_GROUPS = 2
SSD_STATE = 128
SSD_CONV = 4
SSD_CHUNK = 128
SSD_CONV_DIM = SSD_INNER + 2 * SSD_GROUPS * SSD_STATE

CONF_WIDTH = 512
CONF_KERNEL = 31

ATTN_HEADS = 8
ATTN_KV_HEADS = 2
ATTN_HEAD_DIM = 64
ATTN_WIDTH = ATTN_HEADS * ATTN_HEAD_DIM
ATTN_KV_WIDTH = ATTN_KV_HEADS * ATTN_HEAD_DIM
WINDOW = 128
ATTN_BLOCK = 128

N_BUCKETS = 32
MAX_DISTANCE = 128

PLE_DIM = 256
MIX_WIDTH = SSD_INNER + CONF_WIDTH + ATTN_WIDTH
EPS = 1e-6

SPLIT_SIZES = (
    SSD_INNER,
    SSD_CONV_DIM,
    SSD_HEADS,
    2 * CONF_WIDTH,
    CONF_WIDTH,
    ATTN_WIDTH,
    ATTN_KV_WIDTH,
    ATTN_KV_WIDTH,
    ATTN_WIDTH,
)
IN_WIDTH = sum(SPLIT_SIZES)

kernel_name = "hybrid_ssd_conformer_swa_parallel_block"


def rms_norm(x, w):
    xf = x.astype(jnp.float32)
    y = xf * lax.rsqrt(jnp.mean(xf * xf, axis=-1, keepdims=True) + EPS)
    return (y * w.astype(jnp.float32)).astype(x.dtype)


def layer_norm(x, w, b):
    xf = x.astype(jnp.float32)
    mu = jnp.mean(xf, axis=-1, keepdims=True)
    xc = xf - mu
    y = xc * lax.rsqrt(jnp.mean(xc * xc, axis=-1, keepdims=True) + EPS)
    return (y * w.astype(jnp.float32) + b.astype(jnp.float32)).astype(x.dtype)


def causal_depthwise_conv(u, w, b):
    k, c = w.shape
    out = lax.conv_general_dilated(
        u, w[:, None, :].astype(u.dtype), window_strides=(1,), padding=[(k - 1, 0)],
        dimension_numbers=('NWC', 'WIO', 'NWC'), feature_group_count=c)
    return out + b.astype(u.dtype)


def split_columns(u):
    parts, start = [], 0
    for size in SPLIT_SIZES:
        parts.append(u[..., start:start + size])
        start += size
    return parts


def ssd_mixer(z, xbc, dt_raw, conv_w, conv_b, dt_bias, a_log, d_skip, norm_w):
    bsz, seq, _ = xbc.shape
    nc = seq // SSD_CHUNK
    hpg = SSD_HEADS // SSD_GROUPS
    xbc = jax.nn.silu(causal_depthwise_conv(xbc, conv_w, conv_b))
    xs = xbc[..., :SSD_INNER]
    bm = xbc[..., SSD_INNER:SSD_INNER + SSD_GROUPS * SSD_STATE]
    cm = xbc[..., SSD_INNER + SSD_GROUPS * SSD_STATE:]
    x = xs.reshape(bsz, nc, SSD_CHUNK, SSD_GROUPS, hpg, SSD_HEAD_DIM)
    bm = bm.reshape(bsz, nc, SSD_CHUNK, SSD_GROUPS, SSD_STATE)
    cm = cm.reshape(bsz, nc, SSD_CHUNK, SSD_GROUPS, SSD_STATE)
    dt = jax.nn.softplus(dt_raw.astype(jnp.float32) + dt_bias.astype(jnp.float32))
    a = -jnp.exp(a_log.astype(jnp.float32))
    dt = dt.reshape(bsz, nc, SSD_CHUNK, SSD_GROUPS, hpg)
    da = dt * a.reshape(SSD_GROUPS, hpg)
    a_cs = jnp.cumsum(jnp.transpose(da, (0, 1, 3, 4, 2)), axis=-1)
    xdt = x * dt[..., None]
    seg = a_cs[..., :, None] - a_cs[..., None, :]
    causal = jnp.tril(jnp.ones((SSD_CHUNK, SSD_CHUNK), dtype=bool))
    lmat = jnp.exp(jnp.where(causal, seg, -jnp.inf))
    cb = jnp.einsum('bclgn,bcsgn->bcgls', cm, bm)
    y_diag = jnp.einsum('bcgls,bcgrls,bcsgrp->bclgrp', cb, lmat, xdt)
    decay_states = jnp.exp(a_cs[..., -1:] - a_cs)
    states = jnp.einsum('bclgn,bcgrl,bclgrp->bcgrpn', bm, decay_states, xdt)
    a_tot = a_cs[..., -1]

    def step(h, inp):
        st, dec = inp
        return h * jnp.exp(dec)[..., None, None] + st, h

    h0 = jnp.zeros((bsz, SSD_GROUPS, hpg, SSD_HEAD_DIM, SSD_STATE), states.dtype)
    _, h_in = lax.scan(step, h0, (jnp.moveaxis(states, 1, 0), jnp.moveaxis(a_tot, 1, 0)))
    h_in = jnp.moveaxis(h_in, 0, 1)
    y_off = jnp.einsum('bclgn,bcgrpn,bcgrl->bclgrp', cm, h_in, jnp.exp(a_cs))
    y = y_diag + y_off + x * d_skip.reshape(SSD_GROUPS, hpg)[:, :, None]
    y = y.reshape(bsz, seq, SSD_INNER)
    yg = (y * jax.nn.silu(z.astype(y.dtype))).reshape(bsz, seq, SSD_GROUPS, SSD_INNER // SSD_GROUPS)
    yg = yg.astype(jnp.float32)
    yg = yg * lax.rsqrt(jnp.mean(yg * yg, axis=-1, keepdims=True) + EPS)
    return yg.reshape(bsz, seq, SSD_INNER) * norm_w.astype(jnp.float32)


def conformer_conv(u, dw_w, dw_b, ln_w, ln_b):
    glu = u[..., :CONF_WIDTH] * jax.nn.sigmoid(u[..., CONF_WIDTH:])
    h = causal_depthwise_conv(glu, dw_w, dw_b)
    return jax.nn.silu(layer_norm(h, ln_w, ln_b))


def t5_bucket(dist):
    max_exact = N_BUCKETS // 2
    d = jnp.maximum(dist, 0)
    large = max_exact + (jnp.log(jnp.maximum(d, 1).astype(jnp.float32) / max_exact)
                         / math.log(MAX_DISTANCE / max_exact) * (N_BUCKETS - max_exact)).astype(jnp.int32)
    large = jnp.minimum(large, N_BUCKETS - 1)
    return jnp.where(d < max_exact, d, large)


def sliding_window_attention(q, k, v, sinks, rel_bias):
    bsz, seq, _ = q.shape
    nb = seq // ATTN_BLOCK
    grp = ATTN_HEADS // ATTN_KV_HEADS
    q = q.reshape(bsz, nb, ATTN_BLOCK, ATTN_KV_HEADS, grp, ATTN_HEAD_DIM)
    k = k.reshape(bsz, seq, ATTN_KV_HEADS, ATTN_HEAD_DIM)
    v = v.reshape(bsz, seq, ATTN_KV_HEADS, ATTN_HEAD_DIM)
    pad = ((0, 0), (ATTN_BLOCK, 0), (0, 0), (0, 0))

    def band(t):
        prev = jnp.pad(t, pad)[:, :seq].reshape(bsz, nb, ATTN_BLOCK, ATTN_KV_HEADS, ATTN_HEAD_DIM)
        cur = t.reshape(bsz, nb, ATTN_BLOCK, ATTN_KV_HEADS, ATTN_HEAD_DIM)
        return jnp.concatenate([prev, cur], axis=2)

    kb, vb = band(k), band(v)
    s = jnp.einsum('bnqkgd,bnskd->bnkgqs', q, kb).astype(jnp.float32) * (ATTN_HEAD_DIM ** -0.5)
    q_idx = jnp.arange(ATTN_BLOCK)[:, None]
    s_idx = jnp.arange(2 * ATTN_BLOCK)[None, :]
    dist = q_idx + ATTN_BLOCK - s_idx
    bias = rel_bias.astype(jnp.float32)[t5_bucket(dist)]
    bias = jnp.transpose(bias, (2, 0, 1)).reshape(ATTN_KV_HEADS, grp, ATTN_BLOCK, 2 * ATTN_BLOCK)
    blk = jnp.arange(nb)[:, None, None]
    valid = (dist >= 0) & (dist < WINDOW) & ((blk > 0) | (s_idx >= ATTN_BLOCK))
    s = jnp.where(valid[None, :, None, None], s + bias, -jnp.inf)
    sink = sinks.astype(jnp.float32).reshape(1, 1, ATTN_KV_HEADS, grp, 1, 1)
    m = jnp.maximum(jnp.max(s, axis=-1, keepdims=True), sink)
    e = jnp.exp(s - m)
    probs = e / (jnp.sum(e, axis=-1, keepdims=True) + jnp.exp(sink - m))
    o = jnp.einsum('bnkgqs,bnskd->bnqkgd', probs.astype(vb.dtype), vb)
    return o.reshape(bsz, seq, ATTN_WIDTH)


def hybrid_layer(x, p_i, pre_norm_w, w_in, ssd_conv_w, ssd_conv_b, ssd_dt_bias, ssd_a_log, ssd_d,
                 ssd_norm_w, conf_dw_w, conf_dw_b, conf_ln_w, conf_ln_b, attn_sinks, rel_bias,
                 w_out, post_norm_w, ple_proj, ple_gate):
    hn = rms_norm(x, pre_norm_w)
    u = jnp.einsum('bld,de->ble', hn, w_in)
    z, xbc, dt_raw, conf_in, conf_gate, q, k, v, attn_gate = split_columns(u)
    y_ssd = ssd_mixer(z, xbc, dt_raw, ssd_conv_w, ssd_conv_b, ssd_dt_bias, ssd_a_log, ssd_d, ssd_norm_w)
    y_conf = conformer_conv(conf_in, conf_dw_w, conf_dw_b, conf_ln_w, conf_ln_b) * jax.nn.silu(conf_gate)
    y_attn = sliding_window_attention(q, k, v, attn_sinks, rel_bias) * jax.nn.silu(attn_gate)
    y = jnp.concatenate([y_ssd.astype(x.dtype), y_conf.astype(x.dtype), y_attn.astype(x.dtype)], axis=-1)
    h = x + rms_norm(jnp.einsum('ble,ed->bld', y, w_out), post_norm_w)
    gate = jax.nn.sigmoid(jnp.einsum('bld,de->ble', h, ple_gate))
    return h + jnp.einsum('blp,pd->bld', p_i, ple_proj) * gate


def setup_inputs(seed: int = 0) -> dict:
    key = jax.random.key(seed)
    ks = jax.random.split(key, 24)
    f32 = jnp.float32
    nrm = lambda k, shape, scale: jax.random.normal(k, shape, f32) * scale
    dt = jnp.exp(jax.random.uniform(ks[6], (DEPTH, SSD_HEADS), f32) * (math.log(0.1) - math.log(0.001)) + math.log(0.001))
    dt_bias = dt + jnp.log(-jnp.expm1(-dt))
    return {
        "x": nrm(ks[0], (BATCH, SEQ, D_MODEL), 1.0),
        "p": nrm(ks[1], (DEPTH, BATCH, SEQ, PLE_DIM), 1.0),
        "pre_norm_w": 1.0 + nrm(ks[2], (DEPTH, D_MODEL), 0.05),
        "w_in": nrm(ks[3], (DEPTH, D_MODEL, IN_WIDTH), D_MODEL ** -0.5),
        "ssd_conv_w": nrm(ks[4], (DEPTH, SSD_CONV, SSD_CONV_DIM), SSD_CONV ** -0.5),
        "ssd_conv_b": nrm(ks[5], (DEPTH, SSD_CONV_DIM), 0.02),
        "ssd_dt_bias": dt_bias,
        "ssd_a_log": jnp.log(jax.random.uniform(ks[7], (DEPTH, SSD_HEADS), f32, 1.0, 16.0)),
        "ssd_d": 1.0 + nrm(ks[8], (DEPTH, SSD_HEADS), 0.1),
        "ssd_norm_w": 1.0 + nrm(ks[9], (DEPTH, SSD_INNER), 0.05),
        "conf_dw_w": nrm(ks[10], (DEPTH, CONF_KERNEL, CONF_WIDTH), CONF_KERNEL ** -0.5),
        "conf_dw_b": nrm(ks[11], (DEPTH, CONF_WIDTH), 0.02),
        "conf_ln_w": 1.0 + nrm(ks[12], (DEPTH, CONF_WIDTH), 0.05),
        "conf_ln_b": nrm(ks[13], (DEPTH, CONF_WIDTH), 0.02),
        "attn_sinks": nrm(ks[14], (DEPTH, ATTN_HEADS), 0.5),
        "rel_bias": nrm(ks[15], (N_BUCKETS, ATTN_HEADS), 0.1),
        "w_out": nrm(ks[16], (DEPTH, MIX_WIDTH, D_MODEL), MIX_WIDTH ** -0.5),
        "post_norm_w": 1.0 + nrm(ks[17], (DEPTH, D_MODEL), 0.05),
        "ple_proj": nrm(ks[18], (DEPTH, PLE_DIM, D_MODEL), PLE_DIM ** -0.5),
        "ple_gate": nrm(ks[19], (DEPTH, D_MODEL, D_MODEL), D_MODEL ** -0.5),
    }


def reference(x, p, pre_norm_w, w_in, ssd_conv_w, ssd_conv_b, ssd_dt_bias, ssd_a_log, ssd_d,
              ssd_norm_w, conf_dw_w, conf_dw_b, conf_ln_w, conf_ln_b, attn_sinks, rel_bias,
              w_out, post_norm_w, ple_proj, ple_gate):
    h = x
    for i in range(DEPTH):
        h = hybrid_layer(h, p[i], pre_norm_w[i], w_in[i], ssd_conv_w[i], ssd_conv_b[i], ssd_dt_bias[i],
                         ssd_a_log[i], ssd_d[i], ssd_norm_w[i], conf_dw_w[i], conf_dw_b[i], conf_ln_w[i],
                         conf_ln_b[i], attn_sinks[i], rel_bias, w_out[i], post_norm_w[i], ple_proj[i],
                         ple_gate[i]).astype(x.dtype)
    return h
```

```python
import functools
import math

import numpy as np
import jax
import jax.numpy as jnp
from jax import lax
from jax.experimental import pallas as pl
from jax.experimental.pallas import tpu as pltpu

F32 = jnp.float32
BF16 = jnp.bfloat16

D_MODEL = 1024
SSD_INNER = 1024
SSD_HEADS = 16
SSD_HEAD_DIM = 64
SSD_GROUPS = 2
SSD_STATE = 128
SSD_CONV = 4
CHUNK = 128
SSD_CONV_DIM = SSD_INNER + 2 * SSD_GROUPS * SSD_STATE
CONF_WIDTH = 512
CONF_KERNEL = 31
ATTN_HEADS = 8
ATTN_KV_HEADS = 2
ATTN_HEAD_DIM = 64
ATTN_WIDTH = ATTN_HEADS * ATTN_HEAD_DIM
N_BUCKETS = 32
MAX_DISTANCE = 128
PLE_DIM = 256
MIX_WIDTH = SSD_INNER + CONF_WIDTH + ATTN_WIDTH
EPS = 1e-6
NEG = -1e30

LANES = 128
SUBLANES = 8
VMEM_LIMIT_BYTES = 56 * 1024 * 1024

SEQ_BLOCK = 256
XBC_PAD = SUBLANES
GLU_PAD = 32
ROW_TILE = 64
CONF_TILE = 32

Z0 = 0
XBC0 = Z0 + SSD_INNER
CIN0 = XBC0 + SSD_CONV_DIM
CG0 = CIN0 + 2 * CONF_WIDTH
Q0 = CG0 + CONF_WIDTH
K0 = Q0 + ATTN_WIDTH
V0 = K0 + 2 * LANES
AG0 = V0 + 2 * LANES
DT0 = AG0 + ATTN_WIDTH
IN_COLS = DT0 + LANES


def _sigmoid(x):
    return 1.0 / (1.0 + jnp.exp(-x))


def _silu(x):
    return x * _sigmoid(x)


def _softplus(x):
    return jnp.maximum(x, 0.0) + jnp.log(1.0 + jnp.exp(-jnp.abs(x)))


def _rows(i, tile):
    return pl.ds(pl.multiple_of(i * tile, tile), tile)


def _split3(x):
    hi = x.astype(BF16)
    r1 = x - hi.astype(F32)
    mid = r1.astype(BF16)
    lo = (r1 - mid.astype(F32)).astype(BF16)
    return hi, mid, lo


def _layer_kernel(x_ref, p_ref, win_ref, wout_ref, wpg_ref, wple_ref,
                  prew_ref, convw_ref, convb_ref, dtb_ref, alog_ref, dskip_ref, ssdnw_ref,
                  cdw_ref, cdb_ref, clnw_ref, clnb_ref, postw_ref, bucket_ref,
                  sinks_ref, relb_ref,
                  o_ref,
                  hn_ref, z_ref, xbc_ref, xc_ref, dt_ref, tmp_ref, glu_ref, sh_ref, cg_ref,
                  q_ref, k_ref, v_ref, ag_ref, y_ref, hb_ref, state_ref, bias_ref):
    T = SEQ_BLOCK
    nchunks = T // CHUNK
    b_idx = pl.program_id(0)
    j_idx = pl.program_id(1)

    lane_lo = lax.broadcasted_iota(jnp.int32, (CHUNK, LANES), 1) < (LANES // 2)

    @pl.when((b_idx == 0) & (j_idx == 0))
    def _():
        bucket = bucket_ref[...]
        qi = lax.broadcasted_iota(jnp.int32, (CHUNK, 2 * CHUNK), 0)
        si = lax.broadcasted_iota(jnp.int32, (CHUNK, 2 * CHUNK), 1)
        dist = qi + CHUNK - si
        in_window = (dist >= 0) & (dist < CHUNK)
        for h in range(ATTN_HEADS):
            acc = jnp.zeros((CHUNK, 2 * CHUNK), F32)
            for bkt in range(N_BUCKETS):
                acc = jnp.where(bucket == bkt, relb_ref[bkt, h], acc)
            bias_ref[h * CHUNK:(h + 1) * CHUNK, :] = jnp.where(in_window, acc, NEG)

    @pl.when(j_idx == 0)
    def _():
        xbc_ref[0:XBC_PAD, :] = jnp.zeros((XBC_PAD, SSD_CONV_DIM), F32)
        glu_ref[0:GLU_PAD, :] = jnp.zeros((GLU_PAD, CONF_WIDTH), F32)
        k_ref[0:CHUNK, :] = jnp.zeros((CHUNK, 2 * LANES), BF16)
        v_ref[0:CHUNK, :] = jnp.zeros((CHUNK, 2 * LANES), BF16)
        state_ref[...] = jnp.zeros(state_ref.shape, F32)

    def prenorm_body(i, c):
        rows = _rows(i, ROW_TILE)
        xv = x_ref[rows, :]
        ms = jnp.mean(xv * xv, axis=-1, keepdims=True)
        hn_ref[rows, :] = (xv * lax.rsqrt(ms + EPS) * prew_ref[...]).astype(BF16)
        return c
    lax.fori_loop(0, T // ROW_TILE, prenorm_body, 0)

    def proj(c0, width):
        return jnp.dot(hn_ref[...], win_ref[:, c0:c0 + width], preferred_element_type=F32)

    z_ref[...] = proj(Z0, SSD_INNER)
    xbc_ref[XBC_PAD:XBC_PAD + T, :] = proj(XBC0, SSD_CONV_DIM)
    tmp_ref[...] = proj(CIN0, 2 * CONF_WIDTH)
    cg_ref[...] = proj(CG0, CONF_WIDTH)
    q_ref[...] = (proj(Q0, ATTN_WIDTH) * (ATTN_HEAD_DIM ** -0.5)).astype(BF16)
    k_ref[CHUNK:CHUNK + T, :] = proj(K0, 2 * LANES).astype(BF16)
    v_ref[CHUNK:CHUNK + T, :] = proj(V0, 2 * LANES).astype(BF16)
    ag_ref[...] = proj(AG0, ATTN_WIDTH)
    dt_ref[...] = proj(DT0, LANES)

    conv_cols = 512
    for rt in range(T // ROW_TILE):
        for ct in range(SSD_CONV_DIM // conv_cols):
            cols = slice(ct * conv_cols, (ct + 1) * conv_cols)
            acc = jnp.broadcast_to(convb_ref[:, cols], (ROW_TILE, conv_cols))
            for k in range(SSD_CONV):
                r0 = XBC_PAD + rt * ROW_TILE - (SSD_CONV - 1) + k
                acc = acc + convw_ref[k:k + 1, cols] * xbc_ref[r0:r0 + ROW_TILE, cols]
            xc_ref[rt * ROW_TILE:(rt + 1) * ROW_TILE, cols] = _silu(acc)

    li = lax.broadcasted_iota(jnp.int32, (CHUNK, CHUNK), 0)
    si_ = lax.broadcasted_iota(jnp.int32, (CHUNK, CHUNK), 1)
    causal = li >= si_
    tril = causal.astype(F32).astype(BF16)
    hpg = SSD_HEADS // SSD_GROUPS
    gw = SSD_INNER // SSD_GROUPS

    def ssd_body(c, carry):
        rows = _rows(c, CHUNK)
        dt = _softplus(dt_ref[rows, :] + dtb_ref[...])
        a = -jnp.exp(alog_ref[...])
        da = dt * a
        hi, mid, lo = _split3(da)
        a_cs = (jnp.dot(tril, hi, preferred_element_type=F32)
                + jnp.dot(tril, mid, preferred_element_type=F32)
                + jnp.dot(tril, lo, preferred_element_type=F32))
        a_cs_t = a_cs.T
        a_tot = a_cs[CHUNK - 1:CHUNK, :]
        p_all = jnp.exp(a_cs)
        q_all = dt * jnp.exp(a_tot - a_cs)
        zc = z_ref[rows, :]
        for g in range(SSD_GROUPS):
            bm = xc_ref[rows, SSD_INNER + g * SSD_STATE:SSD_INNER + (g + 1) * SSD_STATE]
            cm = xc_ref[rows, SSD_INNER + (SSD_GROUPS + g) * SSD_STATE:
                        SSD_INNER + (SSD_GROUPS + g + 1) * SSD_STATE]
            bmb = bm.astype(BF16)
            cmb = cm.astype(BF16)
            cb = lax.dot_general(cmb, bmb, (((1,), (1,)), ((), ())), preferred_element_type=F32)
            s_prev = state_ref[g]
            y_off = jnp.dot(cmb, s_prev.astype(BF16), preferred_element_type=F32)
            y_pairs, xd_pairs, dec_pairs = [], [], []
            for pr in range(hpg // 2):
                h0 = g * hpg + 2 * pr
                h1 = h0 + 1
                c0 = h0 * SSD_HEAD_DIM
                xpair = xc_ref[rows, c0:c0 + LANES]
                lhs = []
                for h in (h0, h1):
                    seg = a_cs[:, h:h + 1] - a_cs_t[h:h + 1, :]
                    lmat = jnp.exp(jnp.where(causal, seg, NEG))
                    lhs.append((cb * lmat).astype(BF16))
                lhs = jnp.concatenate(lhs, axis=1)
                dtp = jnp.where(lane_lo, dt[:, h0:h0 + 1], dt[:, h1:h1 + 1])
                pp = jnp.where(lane_lo, p_all[:, h0:h0 + 1], p_all[:, h1:h1 + 1])
                qp = jnp.where(lane_lo, q_all[:, h0:h0 + 1], q_all[:, h1:h1 + 1])
                xdt = xpair * dtp
                rhs = jnp.concatenate([jnp.where(lane_lo, xdt, 0.0),
                                       jnp.where(lane_lo, 0.0, xdt)], axis=0).astype(BF16)
                y_diag = jnp.dot(lhs, rhs, preferred_element_type=F32)
                y_pairs.append(y_diag + y_off[:, pr * LANES:(pr + 1) * LANES] * pp
                               + xpair * dskip_ref[:, c0:c0 + LANES])
                xd_pairs.append((xpair * qp).astype(BF16))
                dec_pairs.append(pp[CHUNK - 1:CHUNK, :])
            xd = jnp.concatenate(xd_pairs, axis=1)
            dec = jnp.concatenate(dec_pairs, axis=1)
            state_ref[g] = s_prev * dec + jnp.dot(bm.T.astype(BF16), xd, preferred_element_type=F32)
            yg = jnp.concatenate(y_pairs, axis=1) * _silu(zc[:, g * gw:(g + 1) * gw])
            ms = jnp.mean(yg * yg, axis=-1, keepdims=True)
            y_ref[rows, g * gw:(g + 1) * gw] = (
                yg * lax.rsqrt(ms + EPS) * ssdnw_ref[:, g * gw:(g + 1) * gw]).astype(BF16)
        return carry
    lax.fori_loop(0, nchunks, ssd_body, 0)

    def glu_body(i, c):
        rows = _rows(i, ROW_TILE)
        av = tmp_ref[rows, 0:CONF_WIDTH]
        gv = tmp_ref[rows, CONF_WIDTH:2 * CONF_WIDTH]
        glu_ref[pl.ds(pl.multiple_of(GLU_PAD + i * ROW_TILE, GLU_PAD), ROW_TILE), :] = av * _sigmoid(gv)
        return c
    lax.fori_loop(0, T // ROW_TILE, glu_body, 0)

    sh_rows = T + GLU_PAD - SUBLANES
    for r in range(1, SUBLANES):
        sh_ref[r - 1, :, :] = glu_ref[r:r + sh_rows, :]

    def conf_body(i, c):
        base = i * CONF_TILE
        rows = _rows(i, CONF_TILE)
        acc = jnp.broadcast_to(cdb_ref[...], (CONF_TILE, CONF_WIDTH))
        for k in range(CONF_KERNEL):
            off = GLU_PAD - (CONF_KERNEL - 1) + k
            qo, ro = divmod(off, SUBLANES)
            src_rows = pl.ds(pl.multiple_of(base + qo * SUBLANES, SUBLANES), CONF_TILE)
            src = glu_ref[src_rows, :] if ro == 0 else sh_ref[ro - 1, src_rows, :]
            acc = acc + cdw_ref[k:k + 1, :] * src
        mu = jnp.mean(acc, axis=-1, keepdims=True)
        xc = acc - mu
        var = jnp.mean(xc * xc, axis=-1, keepdims=True)
        yl = xc * lax.rsqrt(var + EPS) * clnw_ref[...] + clnb_ref[...]
        y_ref[rows, SSD_INNER:SSD_INNER + CONF_WIDTH] = (_silu(yl) * _silu(cg_ref[rows, :])).astype(BF16)
        return c
    lax.fori_loop(0, T // CONF_TILE, conf_body, 0)

    grp = ATTN_HEADS // ATTN_KV_HEADS
    prev_half = lax.broadcasted_iota(jnp.int32, (CHUNK, 2 * CHUNK), 1) < CHUNK

    def attn_body(i, carry):
        rows = _rows(i, CHUNK)
        band = pl.ds(pl.multiple_of(i * CHUNK, CHUNK), 2 * CHUNK)
        first_pen = jnp.where((j_idx == 0) & (i == 0), NEG, 0.0)
        for kh in range(ATTN_KV_HEADS):
            kb = k_ref[band, kh * LANES:(kh + 1) * LANES]
            vb = v_ref[band, kh * LANES:(kh + 1) * LANES]
            qs = []
            for pr in range(grp // 2):
                c0 = (kh * grp + 2 * pr) * ATTN_HEAD_DIM
                qpair = q_ref[rows, c0:c0 + LANES].astype(F32)
                qs.append(jnp.where(lane_lo, qpair, 0.0).astype(BF16))
                qs.append(jnp.where(lane_lo, 0.0, qpair).astype(BF16))
            qm = jnp.concatenate(qs, axis=0)
            s = lax.dot_general(qm, kb, (((1,), (1,)), ((), ())), preferred_element_type=F32)
            es, inv = [], []
            for gi in range(grp):
                h = kh * grp + gi
                sh = s[gi * CHUNK:(gi + 1) * CHUNK, :] + bias_ref[h * CHUNK:(h + 1) * CHUNK, :]
                sh = sh + jnp.where(prev_half, first_pen, 0.0)
                sink = sinks_ref[h]
                m = jnp.maximum(jnp.max(sh, axis=-1, keepdims=True), sink)
                e = jnp.exp(sh - m)
                denom = jnp.sum(e, axis=-1, keepdims=True) + jnp.exp(sink - m)
                es.append(e.astype(BF16))
                inv.append(1.0 / denom)
            o = jnp.dot(jnp.concatenate(es, axis=0), vb, preferred_element_type=F32)
            outs = []
            for pr in range(grp // 2):
                o0 = o[(2 * pr) * CHUNK:(2 * pr + 1) * CHUNK, :] * inv[2 * pr]
                o1 = o[(2 * pr + 1) * CHUNK:(2 * pr + 2) * CHUNK, :] * inv[2 * pr + 1]
                outs.append(jnp.where(lane_lo, o0, o1))
            oc = jnp.concatenate(outs, axis=1)
            c0 = kh * grp * ATTN_HEAD_DIM
            gate = _silu(ag_ref[rows, c0:c0 + grp * ATTN_HEAD_DIM])
            y_ref[rows, SSD_INNER + CONF_WIDTH + c0:SSD_INNER + CONF_WIDTH + c0 + grp * ATTN_HEAD_DIM] = (
                oc * gate).astype(BF16)
        return carry
    lax.fori_loop(0, nchunks, attn_body, 0)

    tmp_ref[...] = jnp.dot(y_ref[...], wout_ref[...], preferred_element_type=F32)

    def post_body(i, c):
        rows = _rows(i, ROW_TILE)
        ov = tmp_ref[rows, :]
        ms = jnp.mean(ov * ov, axis=-1, keepdims=True)
        hv = x_ref[rows, :] + ov * lax.rsqrt(ms + EPS) * postw_ref[...]
        o_ref[rows, :] = hv
        hb_ref[rows, :] = hv.astype(BF16)
        return c
    lax.fori_loop(0, T // ROW_TILE, post_body, 0)

    tmp_ref[...] = jnp.dot(hb_ref[...], wpg_ref[...], preferred_element_type=F32)
    z_ref[...] = jnp.dot(p_ref[...].astype(BF16), wple_ref[...], preferred_element_type=F32)

    def ple_body(i, c):
        rows = _rows(i, ROW_TILE)
        o_ref[rows, :] = o_ref[rows, :] + z_ref[rows, :] * _sigmoid(tmp_ref[rows, :])
        return c
    lax.fori_loop(0, T // ROW_TILE, ple_body, 0)

    xbc_ref[0:XBC_PAD, :] = xbc_ref[T:T + XBC_PAD, :]
    glu_ref[0:GLU_PAD, :] = glu_ref[T:T + GLU_PAD, :]
    k_ref[0:CHUNK, :] = k_ref[T:T + CHUNK, :]
    v_ref[0:CHUNK, :] = v_ref[T:T + CHUNK, :]


def _t5_bucket_table():
    q = np.arange(CHUNK)[:, None]
    s = np.arange(2 * CHUNK)[None, :]
    d = np.maximum(q + CHUNK - s, 0)
    max_exact = N_BUCKETS // 2
    large = max_exact + (np.log(np.maximum(d, 1).astype(np.float32) / max_exact)
                         / math.log(MAX_DISTANCE / max_exact) * (N_BUCKETS - max_exact)).astype(np.int32)
    large = np.minimum(large, N_BUCKETS - 1)
    return np.where(d < max_exact, d, large).astype(np.int32)


def _reorder_w_in(w_in):
    sizes = (SSD_INNER, SSD_CONV_DIM, SSD_HEADS, 2 * CONF_WIDTH, CONF_WIDTH,
             ATTN_WIDTH, ATTN_KV_HEADS * ATTN_HEAD_DIM, ATTN_KV_HEADS * ATTN_HEAD_DIM, ATTN_WIDTH)
    parts, start = [], 0
    for sz in sizes:
        parts.append(w_in[:, start:start + sz])
        start += sz
    z, xbc, dt, cin, cg, q, k, v, ag = parts

    def dup(w):
        heads = [w[:, i * ATTN_HEAD_DIM:(i + 1) * ATTN_HEAD_DIM] for i in range(ATTN_KV_HEADS)]
        return jnp.concatenate([t for hd in heads for t in (hd, hd)], axis=1)

    dt_pad = jnp.pad(dt, ((0, 0), (0, LANES - SSD_HEADS)))
    return jnp.concatenate([z, xbc, cin, cg, q, dup(k), dup(v), ag, dt_pad], axis=1).astype(BF16)


def _row(v, width=None):
    v = v.reshape(1, -1).astype(F32)
    if width is not None and v.shape[1] < width:
        v = jnp.pad(v, ((0, 0), (0, width - v.shape[1])))
    return v


def _full_spec(shape):
    return pl.BlockSpec(shape, lambda b, j: (0,) * len(shape))


def _weight_spec(shape):
    return pl.BlockSpec(shape, lambda b, j: (0,) * len(shape), pipeline_mode=pl.Buffered(1))


def _layer(x, p_i, w_in, w_out, w_pg, w_ple, pre_w, conv_w, conv_b, dt_bias, a_log, d_skip, ssd_nw,
           cdw, cdb, clnw, clnb, post_w, sinks, rel_bias, bucket):
    bsz, seq, _ = x.shape
    T = SEQ_BLOCK
    assert seq % T == 0 and T % CHUNK == 0

    small = [
        _row(pre_w), conv_w.astype(F32), _row(conv_b), _row(dt_bias, LANES), _row(a_log, LANES),
        _row(jnp.repeat(d_skip, SSD_HEAD_DIM)), _row(ssd_nw),
        jnp.pad(cdw.astype(F32), ((0, GLU_PAD - CONF_KERNEL), (0, 0))), _row(cdb), _row(clnw), _row(clnb),
        _row(post_w), bucket,
    ]
    weights = [_reorder_w_in(w_in), w_out.astype(BF16), w_pg.astype(BF16), w_ple.astype(BF16)]

    in_specs = (
        [pl.BlockSpec((None, T, D_MODEL), lambda b, j: (b, j, 0)),
         pl.BlockSpec((None, T, PLE_DIM), lambda b, j: (b, j, 0))]
        + [_weight_spec(w.shape) for w in weights]
        + [_full_spec(s.shape) for s in small]
        + [pl.BlockSpec(memory_space=pltpu.SMEM), pl.BlockSpec(memory_space=pltpu.SMEM)]
    )
    scratch = [
        pltpu.VMEM((T, D_MODEL), BF16),
        pltpu.VMEM((T, SSD_INNER), F32),
        pltpu.VMEM((XBC_PAD + T, SSD_CONV_DIM), F32),
        pltpu.VMEM((T, SSD_CONV_DIM), F32),
        pltpu.VMEM((T, LANES), F32),
        pltpu.VMEM((T, D_MODEL), F32),
        pltpu.VMEM((GLU_PAD + T, CONF_WIDTH), F32),
        pltpu.VMEM((SUBLANES - 1, T + GLU_PAD - SUBLANES, CONF_WIDTH), F32),
        pltpu.VMEM((T, CONF_WIDTH), F32),
        pltpu.VMEM((T, ATTN_WIDTH), BF16),
        pltpu.VMEM((CHUNK + T, 2 * LANES), BF16),
        pltpu.VMEM((CHUNK + T, 2 * LANES), BF16),
        pltpu.VMEM((T, ATTN_WIDTH), F32),
        pltpu.VMEM((T, MIX_WIDTH), BF16),
        pltpu.VMEM((T, D_MODEL), BF16),
        pltpu.VMEM((SSD_GROUPS, SSD_STATE, SSD_INNER // SSD_GROUPS), F32),
        pltpu.VMEM((ATTN_HEADS * CHUNK, 2 * CHUNK), F32),
    ]
    return pl.pallas_call(
        _layer_kernel,
        out_shape=jax.ShapeDtypeStruct(x.shape, x.dtype),
        grid=(bsz, seq // T),
        in_specs=in_specs,
        out_specs=pl.BlockSpec((None, T, D_MODEL), lambda b, j: (b, j, 0)),
        scratch_shapes=scratch,
        compiler_params=pltpu.CompilerParams(
            dimension_semantics=("arbitrary", "arbitrary"),
            vmem_limit_bytes=VMEM_LIMIT_BYTES),
        name="hybrid_layer",
    )(x, p_i, *weights, *small, sinks.astype(F32), rel_bias.astype(F32))


def kernel(x, p, pre_norm_w, w_in, ssd_conv_w, ssd_conv_b, ssd_dt_bias, ssd_a_log, ssd_d, ssd_norm_w,
           conf_dw_w, conf_dw_b, conf_ln_w, conf_ln_b, attn_sinks, rel_bias, w_out, post_norm_w,
           ple_proj, ple_gate):
    bucket = jnp.asarray(_t5_bucket_table())
    h = x
    for i in range(p.shape[0]):
        h = _layer(h, p[i], w_in[i], w_out[i], ple_gate[i], ple_proj[i], pre_norm_w[i], ssd_conv_w[i],
                   ssd_conv_b[i], ssd_dt_bias[i], ssd_a_log[i], ssd_d[i], ssd_norm_w[i], conf_dw_w[i],
                   conf_dw_b[i], conf_ln_w[i], conf_ln_b[i], post_norm_w[i], attn_sinks[i], rel_bias, bucket)
    return h
```

```python
import functools
import math

import numpy as np
import jax
import jax.numpy as jnp
from jax import lax
from jax.experimental import pallas as pl
from jax.experimental.pallas import tpu as pltpu

F32 = jnp.float32
BF16 = jnp.bfloat16

D_MODEL = 1024
SSD_INNER = 1024
SSD_HEADS = 16
SSD_HEAD_DIM = 64
SSD_GROUPS = 2
SSD_STATE = 128
SSD_CONV = 4
CHUNK = 128
SSD_CONV_DIM = SSD_INNER + 2 * SSD_GROUPS * SSD_STATE
CONF_WIDTH = 512
CONF_KERNEL = 31
ATTN_HEADS = 8
ATTN_KV_HEADS = 2
ATTN_HEAD_DIM = 64
ATTN_WIDTH = ATTN_HEADS * ATTN_HEAD_DIM
N_BUCKETS = 32
MAX_DISTANCE = 128
PLE_DIM = 256
MIX_WIDTH = SSD_INNER + CONF_WIDTH + ATTN_WIDTH
EPS = 1e-6
NEG = -1e30

LANES = 128
SUBLANES = 8
VMEM_LIMIT_BYTES = 56 * 1024 * 1024

CHUNKS_PER_STEP = 2
XBC_PAD = SUBLANES
GLU_PAD = 32
SSD_CONV_TILE = 64
CONF_TILE = 32

Z0 = 0
XBC0 = Z0 + SSD_INNER
CIN0 = XBC0 + SSD_CONV_DIM
CG0 = CIN0 + 2 * CONF_WIDTH
Q0 = CG0 + CONF_WIDTH
K0 = Q0 + ATTN_WIDTH
V0 = K0 + 2 * LANES
AG0 = V0 + 2 * LANES
DT0 = AG0 + ATTN_WIDTH
IN_COLS = DT0 + LANES

_SET = (
    ("hn", (CHUNK, D_MODEL), BF16),
    ("z", (CHUNK, SSD_INNER), F32),
    ("xbc", (XBC_PAD + CHUNK, SSD_CONV_DIM), F32),
    ("xc", (CHUNK, SSD_CONV_DIM), F32),
    ("dt", (CHUNK, LANES), F32),
    ("glu", (GLU_PAD + CHUNK, CONF_WIDTH), F32),
    ("sh", (SUBLANES - 1, CHUNK + GLU_PAD - SUBLANES, CONF_WIDTH), F32),
    ("cg", (CHUNK, CONF_WIDTH), F32),
    ("q", (CHUNK, ATTN_WIDTH), BF16),
    ("k", (2 * CHUNK, 2 * LANES), BF16),
    ("v", (2 * CHUNK, 2 * LANES), BF16),
    ("ag", (CHUNK, ATTN_WIDTH), F32),
    ("y", (CHUNK, MIX_WIDTH), BF16),
)
_N_IN = 22


def _sigmoid(x):
    return 1.0 / (1.0 + jnp.exp(-x))


def _silu(x):
    return x * _sigmoid(x)


def _softplus(x):
    return jnp.maximum(x, 0.0) + jnp.log(1.0 + jnp.exp(-jnp.abs(x)))


def _split3(x):
    hi = x.astype(BF16)
    r1 = x - hi.astype(F32)
    mid = r1.astype(BF16)
    lo = (r1 - mid.astype(F32)).astype(BF16)
    return hi, mid, lo


def _dot(a, b):
    return jnp.dot(a, b, preferred_element_type=F32)


def _dot_nt(a, b):
    return lax.dot_general(a, b, (((1,), (1,)), ((), ())), preferred_element_type=F32)


class _Refs:
    def __init__(self, **kw):
        self.__dict__.update(kw)


def _in_proj(xv, u, w):
    ms = jnp.mean(xv * xv, axis=-1, keepdims=True)
    u.hn[...] = (xv * lax.rsqrt(ms + EPS) * w.prew[...]).astype(BF16)

    def proj(c0, width):
        return _dot(u.hn[...], w.win[:, c0:c0 + width])

    cin = proj(CIN0, 2 * CONF_WIDTH)
    u.glu[GLU_PAD:GLU_PAD + CHUNK, :] = cin[:, :CONF_WIDTH] * _sigmoid(cin[:, CONF_WIDTH:])
    u.cg[...] = proj(CG0, CONF_WIDTH)
    u.xbc[XBC_PAD:XBC_PAD + CHUNK, :] = proj(XBC0, SSD_CONV_DIM)
    u.dt[...] = proj(DT0, LANES)
    u.z[...] = proj(Z0, SSD_INNER)
    u.q[...] = (proj(Q0, ATTN_WIDTH) * (ATTN_HEAD_DIM ** -0.5)).astype(BF16)
    u.ag[...] = proj(AG0, ATTN_WIDTH)
    u.k[CHUNK:2 * CHUNK, :] = proj(K0, 2 * LANES).astype(BF16)
    u.v[CHUNK:2 * CHUNK, :] = proj(V0, 2 * LANES).astype(BF16)


def _hand_over_history(u, o):
    o.xbc[0:XBC_PAD, :] = u.xbc[CHUNK:CHUNK + XBC_PAD, :]
    o.glu[0:GLU_PAD, :] = u.glu[CHUNK:CHUNK + GLU_PAD, :]
    o.k[0:CHUNK, :] = u.k[CHUNK:2 * CHUNK, :]
    o.v[0:CHUNK, :] = u.v[CHUNK:2 * CHUNK, :]


def _ssd(u, w, state_ref):
    lane_lo = lax.broadcasted_iota(jnp.int32, (CHUNK, LANES), 1) < (LANES // 2)
    conv_cols = 512
    for rt in range(CHUNK // SSD_CONV_TILE):
        for ct in range(SSD_CONV_DIM // conv_cols):
            cols = slice(ct * conv_cols, (ct + 1) * conv_cols)
            acc = jnp.broadcast_to(w.convb[:, cols], (SSD_CONV_TILE, conv_cols))
            for k in range(SSD_CONV):
                r0 = XBC_PAD + rt * SSD_CONV_TILE - (SSD_CONV - 1) + k
                acc = acc + w.convw[k:k + 1, cols] * u.xbc[r0:r0 + SSD_CONV_TILE, cols]
            u.xc[rt * SSD_CONV_TILE:(rt + 1) * SSD_CONV_TILE, cols] = _silu(acc)

    li = lax.broadcasted_iota(jnp.int32, (CHUNK, CHUNK), 0)
    si = lax.broadcasted_iota(jnp.int32, (CHUNK, CHUNK), 1)
    causal = li >= si
    tril = causal.astype(F32).astype(BF16)
    hpg = SSD_HEADS // SSD_GROUPS
    gw = SSD_INNER // SSD_GROUPS

    dt = _softplus(u.dt[...] + w.dtb[...])
    a = -jnp.exp(w.alog[...])
    hi, mid, lo = _split3(dt * a)
    a_cs = _dot(tril, hi) + _dot(tril, mid) + _dot(tril, lo)
    a_cs_t = a_cs.T[0:SSD_HEADS, :]
    dt_t = dt.T[0:SSD_HEADS, :]
    q_t = dt_t * jnp.exp(a_cs_t[:, CHUNK - 1:CHUNK] - a_cs_t)
    for g in range(SSD_GROUPS):
        bm = u.xc[:, SSD_INNER + g * SSD_STATE:SSD_INNER + (g + 1) * SSD_STATE]
        cm = u.xc[:, SSD_INNER + (SSD_GROUPS + g) * SSD_STATE:SSD_INNER + (SSD_GROUPS + g + 1) * SSD_STATE]
        cmb = cm.astype(BF16)
        cb = _dot_nt(cmb, bm.astype(BF16))
        bm_t = bm.T
        s_prev = state_ref[g]
        y_off = _dot(cmb, s_prev.astype(BF16))
        y_pairs, st_pairs, dec_pairs = [], [], []
        for pr in range(hpg // 2):
            h0 = g * hpg + 2 * pr
            c0 = h0 * SSD_HEAD_DIM
            xpair = u.xc[:, c0:c0 + LANES]
            rhs = jnp.concatenate([jnp.where(lane_lo, xpair, 0.0),
                                   jnp.where(lane_lo, 0.0, xpair)], axis=0).astype(BF16)
            tops, bots, pbs = [], [], []
            for h in (h0, h0 + 1):
                col = jnp.broadcast_to(a_cs[:, h:h + 1], (CHUNK, CHUNK))
                lmat = jnp.exp(jnp.where(causal, col - a_cs_t[h:h + 1, :], NEG))
                tops.append((cb * lmat * dt_t[h:h + 1, :]).astype(BF16))
                bots.append((bm_t * q_t[h:h + 1, :]).astype(BF16))
                pbs.append(jnp.exp(col))
            lhs = jnp.concatenate([jnp.concatenate(tops, axis=1), jnp.concatenate(bots, axis=1)], axis=0)
            r = _dot(lhs, rhs)
            pp = jnp.where(lane_lo, pbs[0], pbs[1])
            y_pairs.append(r[0:CHUNK, :] + y_off[:, pr * LANES:(pr + 1) * LANES] * pp
                           + xpair * w.dskip[:, c0:c0 + LANES])
            st_pairs.append(r[CHUNK:2 * CHUNK, :])
            dec_pairs.append(pp[CHUNK - 1:CHUNK, :])
        state_ref[g] = (s_prev * jnp.concatenate(dec_pairs, axis=1) + jnp.concatenate(st_pairs, axis=1))
        yg = jnp.concatenate(y_pairs, axis=1) * _silu(u.z[:, g * gw:(g + 1) * gw])
        ms = jnp.mean(yg * yg, axis=-1, keepdims=True)
        u.y[:, g * gw:(g + 1) * gw] = (yg * lax.rsqrt(ms + EPS) * w.ssdnw[:, g * gw:(g + 1) * gw]).astype(BF16)


def _conformer(u, w):
    sh_rows = CHUNK + GLU_PAD - SUBLANES
    for r in range(1, SUBLANES):
        u.sh[r - 1, :, :] = u.glu[r:r + sh_rows, :]
    for rt in range(CHUNK // CONF_TILE):
        base = rt * CONF_TILE
        acc = jnp.broadcast_to(w.cdb[...], (CONF_TILE, CONF_WIDTH))
        for k in range(CONF_KERNEL):
            qo, ro = divmod(GLU_PAD - (CONF_KERNEL - 1) + k, SUBLANES)
            r0 = base + qo * SUBLANES
            src = u.glu[r0:r0 + CONF_TILE, :] if ro == 0 else u.sh[ro - 1, r0:r0 + CONF_TILE, :]
            acc = acc + w.cdw[k:k + 1, :] * src
        mu = jnp.mean(acc, axis=-1, keepdims=True)
        xc = acc - mu
        var = jnp.mean(xc * xc, axis=-1, keepdims=True)
        yl = xc * lax.rsqrt(var + EPS) * w.clnw[...] + w.clnb[...]
        u.y[base:base + CONF_TILE, SSD_INNER:SSD_INNER + CONF_WIDTH] = (
            _silu(yl) * _silu(u.cg[base:base + CONF_TILE, :])).astype(BF16)


def _attention(u, w, bias_ref, first_pen):
    lane_lo = lax.broadcasted_iota(jnp.int32, (CHUNK, LANES), 1) < (LANES // 2)
    grp = ATTN_HEADS // ATTN_KV_HEADS
    for kh in range(ATTN_KV_HEADS):
        kb = u.k[:, kh * LANES:(kh + 1) * LANES]
        vb = u.v[:, kh * LANES:(kh + 1) * LANES]
        qs = []
        for pr in range(grp // 2):
            c0 = (kh * grp + 2 * pr) * ATTN_HEAD_DIM
            qpair = u.q[:, c0:c0 + LANES].astype(F32)
            qs.append(jnp.where(lane_lo, qpair, 0.0).astype(BF16))
            qs.append(jnp.where(lane_lo, 0.0, qpair).astype(BF16))
        s = _dot_nt(jnp.concatenate(qs, axis=0), kb)
        es, inv = [], []
        for gi in range(grp):
            h = kh * grp + gi
            sh = s[gi * CHUNK:(gi + 1) * CHUNK, :] + bias_ref[h * CHUNK:(h + 1) * CHUNK, :]
            if first_pen is not None:
                prev_half = lax.broadcasted_iota(jnp.int32, (CHUNK, 2 * CHUNK), 1) < CHUNK
                sh = sh + jnp.where(prev_half, first_pen, 0.0)
            sink = w.sinks[h]
            m = jnp.maximum(jnp.max(sh, axis=-1, keepdims=True), sink)
            e = jnp.exp(sh - m)
            denom = jnp.sum(e, axis=-1, keepdims=True) + jnp.exp(sink - m)
            es.append(e.astype(BF16))
            inv.append(1.0 / denom)
        o = _dot(jnp.concatenate(es, axis=0), vb)
        outs = []
        for pr in range(grp // 2):
            o0 = o[(2 * pr) * CHUNK:(2 * pr + 1) * CHUNK, :] * inv[2 * pr]
            o1 = o[(2 * pr + 1) * CHUNK:(2 * pr + 2) * CHUNK, :] * inv[2 * pr + 1]
            outs.append(jnp.where(lane_lo, o0, o1))
        c0 = kh * grp * ATTN_HEAD_DIM
        gate = _silu(u.ag[:, c0:c0 + grp * ATTN_HEAD_DIM])
        u.y[:, SSD_INNER + CONF_WIDTH + c0:SSD_INNER + CONF_WIDTH + c0 + grp * ATTN_HEAD_DIM] = (
            jnp.concatenate(outs, axis=1) * gate).astype(BF16)


def _out_proj(u, w, xv, pv):
    ov = _dot(u.y[...], w.wout[...])
    ms = jnp.mean(ov * ov, axis=-1, keepdims=True)
    hv = xv + ov * lax.rsqrt(ms + EPS) * w.postw[...]
    gate = _sigmoid(_dot(hv.astype(BF16), w.wpg[...]))
    return hv + _dot(pv.astype(BF16), w.wple[...]) * gate


def _layer_kernel(steps_per_seq, *refs):
    (x_ref, xn_ref, p_ref, win, wout, wpg, wple, prew, convw, convb, dtb, alog, dskip, ssdnw,
     cdw, cdb, clnw, clnb, postw, bucket_ref, sinks, relb) = refs[:_N_IN]
    o_ref = refs[_N_IN]
    scratch = refs[_N_IN + 1:]
    n = len(_SET)
    ua = _Refs(**{name: r for (name, _, _), r in zip(_SET, scratch[:n])})
    ub = _Refs(**{name: r for (name, _, _), r in zip(_SET, scratch[n:2 * n])})
    state_ref, bias_ref = scratch[2 * n:]
    w = _Refs(win=win, wout=wout, wpg=wpg, wple=wple, prew=prew, convw=convw, convb=convb, dtb=dtb,
              alog=alog, dskip=dskip, ssdnw=ssdnw, cdw=cdw, cdb=cdb, clnw=clnw, clnb=clnb, postw=postw,
              sinks=sinks)

    step = pl.program_id(0)
    seq_start = lax.rem(step, steps_per_seq) == 0

    @pl.when(step == 0)
    def _():
        bucket = bucket_ref[...]
        qi = lax.broadcasted_iota(jnp.int32, (CHUNK, 2 * CHUNK), 0)
        si = lax.broadcasted_iota(jnp.int32, (CHUNK, 2 * CHUNK), 1)
        dist = qi + CHUNK - si
        in_window = (dist >= 0) & (dist < CHUNK)
        for h in range(ATTN_HEADS):
            acc = jnp.zeros((CHUNK, 2 * CHUNK), F32)
            for bkt in range(N_BUCKETS):
                acc = jnp.where(bucket == bkt, relb[bkt, h], acc)
            bias_ref[h * CHUNK:(h + 1) * CHUNK, :] = jnp.where(in_window, acc, NEG)
        _in_proj(x_ref[0], ua, w)

    @pl.when(seq_start)
    def _():
        ua.xbc[0:XBC_PAD, :] = jnp.zeros((XBC_PAD, SSD_CONV_DIM), F32)
        ua.glu[0:GLU_PAD, :] = jnp.zeros((GLU_PAD, CONF_WIDTH), F32)
        ua.k[0:CHUNK, :] = jnp.zeros((CHUNK, 2 * LANES), BF16)
        ua.v[0:CHUNK, :] = jnp.zeros((CHUNK, 2 * LANES), BF16)
        state_ref[...] = jnp.zeros(state_ref.shape, F32)

    first_pen = jnp.where(seq_start, NEG, 0.0)

    _in_proj(x_ref[1], ub, w)
    _hand_over_history(ua, ub)
    _ssd(ua, w, state_ref)
    _conformer(ua, w)
    _attention(ua, w, bias_ref, first_pen)
    o_ref[0] = _out_proj(ua, w, x_ref[0], p_ref[0])

    _in_proj(xn_ref[...], ua, w)
    _hand_over_history(ub, ua)
    _ssd(ub, w, state_ref)
    _conformer(ub, w)
    _attention(ub, w, bias_ref, None)
    o_ref[1] = _out_proj(ub, w, x_ref[1], p_ref[1])


def _reorder_w_in(w_in):
    sizes = (SSD_INNER, SSD_CONV_DIM, SSD_HEADS, 2 * CONF_WIDTH, CONF_WIDTH,
             ATTN_WIDTH, ATTN_KV_HEADS * ATTN_HEAD_DIM, ATTN_KV_HEADS * ATTN_HEAD_DIM, ATTN_WIDTH)
    parts, start = [], 0
    for sz in sizes:
        parts.append(w_in[:, start:start + sz])
        start += sz
    z, xbc, dt, cin, cg, q, k, v, ag = parts

    def dup(wm):
        heads = [wm[:, i * ATTN_HEAD_DIM:(i + 1) * ATTN_HEAD_DIM] for i in range(ATTN_KV_HEADS)]
        return jnp.concatenate([t for hd in heads for t in (hd, hd)], axis=1)

    dt_pad = jnp.pad(dt, ((0, 0), (0, LANES - SSD_HEADS)))
    return jnp.concatenate([z, xbc, cin, cg, q, dup(k), dup(v), ag, dt_pad], axis=1).astype(BF16)


def _row(v, width=None):
    v = v.reshape(1, -1).astype(F32)
    if width is not None and v.shape[1] < width:
        v = jnp.pad(v, ((0, 0), (0, width - v.shape[1])))
    return v


def _full_spec(shape):
    return pl.BlockSpec(shape, lambda s: (0,) * len(shape))


def _weight_spec(shape):
    return pl.BlockSpec(shape, lambda s: (0,) * len(shape), pipeline_mode=pl.Buffered(1))


def _layer(x, p_i, w_in, w_out, w_pg, w_ple, pre_w, conv_w, conv_b, dt_bias, a_log, d_skip, ssd_nw,
           cdw, cdb, clnw, clnb, post_w, sinks, rel_bias, bucket):
    bsz, seq, _ = x.shape
    step_rows = CHUNKS_PER_STEP * CHUNK
    assert seq % step_rows == 0
    n_chunks = bsz * seq // CHUNK
    n_steps = n_chunks // CHUNKS_PER_STEP
    xc = x.reshape(n_chunks, CHUNK, D_MODEL)
    pc = p_i.reshape(n_chunks, CHUNK, PLE_DIM)

    small = [
        _row(pre_w), conv_w.astype(F32), _row(conv_b), _row(dt_bias, LANES), _row(a_log, LANES),
        _row(jnp.repeat(d_skip, SSD_HEAD_DIM)), _row(ssd_nw),
        jnp.pad(cdw.astype(F32), ((0, GLU_PAD - CONF_KERNEL), (0, 0))), _row(cdb), _row(clnw), _row(clnb),
        _row(post_w), bucket,
    ]
    weights = [_reorder_w_in(w_in), w_out.astype(BF16), w_pg.astype(BF16), w_ple.astype(BF16)]

    in_specs = (
        [pl.BlockSpec((CHUNKS_PER_STEP, CHUNK, D_MODEL), lambda s: (s, 0, 0)),
         pl.BlockSpec((None, CHUNK, D_MODEL),
                      lambda s: (jnp.minimum(CHUNKS_PER_STEP * (s + 1), n_chunks - 1), 0, 0)),
         pl.BlockSpec((CHUNKS_PER_STEP, CHUNK, PLE_DIM), lambda s: (s, 0, 0))]
        + [_weight_spec(wt.shape) for wt in weights]
        + [_full_spec(sm.shape) for sm in small]
        + [pl.BlockSpec(memory_space=pltpu.SMEM), pl.BlockSpec(memory_space=pltpu.SMEM)]
    )
    assert len(in_specs) == _N_IN
    scratch = ([pltpu.VMEM(shape, dtype) for _ in range(CHUNKS_PER_STEP) for (_, shape, dtype) in _SET]
               + [pltpu.VMEM((SSD_GROUPS, SSD_STATE, SSD_INNER // SSD_GROUPS), F32),
                  pltpu.VMEM((ATTN_HEADS * CHUNK, 2 * CHUNK), F32)])
    out = pl.pallas_call(
        functools.partial(_layer_kernel, seq // step_rows),
        out_shape=jax.ShapeDtypeStruct(xc.shape, x.dtype),
        grid=(n_steps,),
        in_specs=in_specs,
        out_specs=pl.BlockSpec((CHUNKS_PER_STEP, CHUNK, D_MODEL), lambda s: (s, 0, 0)),
        scratch_shapes=scratch,
        compiler_params=pltpu.CompilerParams(
            dimension_semantics=("arbitrary",),
            vmem_limit_bytes=VMEM_LIMIT_BYTES),
        name="hybrid_layer",
    )(xc, xc, pc, *weights, *small, sinks.astype(F32), rel_bias.astype(F32))
    return out.reshape(x.shape)


def kernel(x, p, pre_norm_w, w_in, ssd_conv_w, ssd_conv_b, ssd_dt_bias, ssd_a_log, ssd_d, ssd_norm_w,
           conf_dw_w, conf_dw_b, conf_ln_w, conf_ln_b, attn_sinks, rel_bias, w_out, post_norm_w,
           ple_proj, ple_gate):
    bucket = jnp.asarray(_t5_bucket_table())
    h = x
    for i in range(p.shape[0]):
        h = _layer(h, p[i], w_in[i], w_out[i], ple_gate[i], ple_proj[i], pre_norm_w[i], ssd_conv_w[i],
                   ssd_conv_b[i], ssd_dt_bias[i], ssd_a_log[i], ssd_d[i], ssd_norm_w[i], conf_dw_w[i],
                   conf_dw_b[i], conf_ln_w[i], conf_ln_b[i], post_norm_w[i], attn_sinks[i], rel_bias, bucket)
    return h


def _t5_bucket_table():
    q = np.arange(CHUNK)[:, None]
    s = np.arange(2 * CHUNK)[None, :]
    d = np.maximum(q + CHUNK - s, 0)
    max_exact = N_BUCKETS // 2
    large = max_exact + (np.log(np.maximum(d, 1).astype(np.float32) / max_exact)
                         / math.log(MAX_DISTANCE / max_exact) * (N_BUCKETS - max_exact)).astype(np.int32)
    large = np.minimum(large, N_BUCKETS - 1)
    return np.where(d < max_exact, d, large).astype(np.int32)
```

```python
import functools
import math

import numpy as np
import jax
import jax.numpy as jnp
from jax import lax
from jax.experimental import pallas as pl
from jax.experimental.pallas import tpu as pltpu

F32 = jnp.float32
BF16 = jnp.bfloat16

D_MODEL = 1024
SSD_INNER = 1024
SSD_HEADS = 16
SSD_HEAD_DIM = 64
SSD_GROUPS = 2
SSD_STATE = 128
SSD_CONV = 4
CHUNK = 128
SSD_CONV_DIM = SSD_INNER + 2 * SSD_GROUPS * SSD_STATE
CONF_WIDTH = 512
CONF_KERNEL = 31
ATTN_HEADS = 8
ATTN_KV_HEADS = 2
ATTN_HEAD_DIM = 64
ATTN_WIDTH = ATTN_HEADS * ATTN_HEAD_DIM
N_BUCKETS = 32
MAX_DISTANCE = 128
PLE_DIM = 256
MIX_WIDTH = SSD_INNER + CONF_WIDTH + ATTN_WIDTH
EPS = 1e-6
NEG = -1e30

LANES = 128
SUBLANES = 8
VMEM_LIMIT_BYTES = 56 * 1024 * 1024

CHUNKS_PER_STEP = 2
XBC_PAD = SUBLANES
GLU_PAD = 32
SSD_CONV_TILE = 64
CONF_TILE = 32

Z0 = 0
XBC0 = Z0 + SSD_INNER
CIN0 = XBC0 + SSD_CONV_DIM
CG0 = CIN0 + 2 * CONF_WIDTH
Q0 = CG0 + CONF_WIDTH
K0 = Q0 + ATTN_WIDTH
V0 = K0 + 2 * LANES
AG0 = V0 + 2 * LANES
DT0 = AG0 + ATTN_WIDTH
IN_COLS = DT0 + LANES

_SET = (
    ("hn", (CHUNK, D_MODEL), BF16),
    ("z", (CHUNK, SSD_INNER), F32),
    ("xbc", (XBC_PAD + CHUNK, SSD_CONV_DIM), F32),
    ("xc", (CHUNK, SSD_CONV_DIM), F32),
    ("dt", (CHUNK, LANES), F32),
    ("glu", (GLU_PAD + CHUNK, CONF_WIDTH), F32),
    ("sh", (SUBLANES - 1, CHUNK + GLU_PAD - SUBLANES, CONF_WIDTH), F32),
    ("cg", (CHUNK, CONF_WIDTH), F32),
    ("q", (CHUNK, ATTN_WIDTH), BF16),
    ("k", (2 * CHUNK, 2 * LANES), BF16),
    ("v", (2 * CHUNK, 2 * LANES), BF16),
    ("ag", (CHUNK, ATTN_WIDTH), F32),
    ("y", (CHUNK, MIX_WIDTH), BF16),
)
_N_IN = 22


def _sigmoid(x):
    return 1.0 / (1.0 + jnp.exp(-x))


def _silu(x):
    return x * _sigmoid(x)


def _softplus(x):
    return jnp.maximum(x, 0.0) + jnp.log(1.0 + jnp.exp(-jnp.abs(x)))


def _split3(x):
    hi = x.astype(BF16)
    r1 = x - hi.astype(F32)
    mid = r1.astype(BF16)
    lo = (r1 - mid.astype(F32)).astype(BF16)
    return hi, mid, lo


def _dot(a, b):
    return jnp.dot(a, b, preferred_element_type=F32)


def _dot_nt(a, b):
    return lax.dot_general(a, b, (((1,), (1,)), ((), ())), preferred_element_type=F32)


class _Refs:
    def __init__(self, **kw):
        self.__dict__.update(kw)


def _in_proj_tasks(get_x, u, w):
    def prenorm():
        xv = get_x()
        ms = jnp.mean(xv * xv, axis=-1, keepdims=True)
        u.hn[...] = (xv * lax.rsqrt(ms + EPS) * w.prew[...]).astype(BF16)

    def proj(c0, width):
        return _dot(u.hn[...], w.win[:, c0:c0 + width])

    def glu(half):
        def run():
            c0 = half * (CONF_WIDTH // 2)
            av = proj(CIN0 + c0, CONF_WIDTH // 2)
            gv = proj(CIN0 + CONF_WIDTH + c0, CONF_WIDTH // 2)
            u.glu[GLU_PAD:GLU_PAD + CHUNK, c0:c0 + CONF_WIDTH // 2] = av * _sigmoid(gv)
        return run

    def seg(c0, width, store):
        return lambda: store(proj(c0, width))

    def store_to(ref, r0, c0, width, cast=None, scale=None):
        def store(val):
            if scale is not None:
                val = val * scale
            ref[r0:r0 + CHUNK, c0:c0 + width] = val if cast is None else val.astype(cast)
        return store

    seg_w = 512
    tasks = [prenorm, glu(0), glu(1), seg(CG0, CONF_WIDTH, store_to(u.cg, 0, 0, CONF_WIDTH))]
    for c0 in range(0, SSD_CONV_DIM, seg_w):
        tasks.append(seg(XBC0 + c0, seg_w, store_to(u.xbc, XBC_PAD, c0, seg_w)))
    tasks.append(seg(DT0, LANES, store_to(u.dt, 0, 0, LANES)))
    for c0 in range(0, SSD_INNER, seg_w):
        tasks.append(seg(Z0 + c0, seg_w, store_to(u.z, 0, c0, seg_w)))
    tasks.append(seg(Q0, ATTN_WIDTH, store_to(u.q, 0, 0, ATTN_WIDTH, cast=BF16, scale=ATTN_HEAD_DIM ** -0.5)))
    tasks.append(seg(AG0, ATTN_WIDTH, store_to(u.ag, 0, 0, ATTN_WIDTH)))
    tasks.append(seg(K0, 2 * LANES, store_to(u.k, CHUNK, 0, 2 * LANES, cast=BF16)))
    tasks.append(seg(V0, 2 * LANES, store_to(u.v, CHUNK, 0, 2 * LANES, cast=BF16)))
    return tasks


def _hand_over_history(u, o):
    o.xbc[0:XBC_PAD, :] = u.xbc[CHUNK:CHUNK + XBC_PAD, :]
    o.glu[0:GLU_PAD, :] = u.glu[CHUNK:CHUNK + GLU_PAD, :]
    o.k[0:CHUNK, :] = u.k[CHUNK:2 * CHUNK, :]
    o.v[0:CHUNK, :] = u.v[CHUNK:2 * CHUNK, :]


def _rep_rows(v, rows):
    return jnp.concatenate([v] * (rows // SUBLANES), axis=0)


def _ssd_tasks(u, w, state_ref):
    tasks = []
    ctx = {}
    lane_lo = lax.broadcasted_iota(jnp.int32, (CHUNK, LANES), 1) < (LANES // 2)
    hpg = SSD_HEADS // SSD_GROUPS
    gw = SSD_INNER // SSD_GROUPS
    conv_cols = 512

    def conv(rt, ct):
        def run():
            cols = slice(ct * conv_cols, (ct + 1) * conv_cols)
            acc = _rep_rows(w.convb[:, cols], SSD_CONV_TILE)
            for k in range(SSD_CONV):
                r0 = XBC_PAD + rt * SSD_CONV_TILE - (SSD_CONV - 1) + k
                acc = acc + _rep_rows(w.convw[k, :, cols], SSD_CONV_TILE) * u.xbc[r0:r0 + SSD_CONV_TILE, cols]
            u.xc[rt * SSD_CONV_TILE:(rt + 1) * SSD_CONV_TILE, cols] = _silu(acc)
        return run
    for rt in range(CHUNK // SSD_CONV_TILE):
        for ct in range(SSD_CONV_DIM // conv_cols):
            tasks.append(conv(rt, ct))

    def prep():
        li = lax.broadcasted_iota(jnp.int32, (CHUNK, CHUNK), 0)
        si = lax.broadcasted_iota(jnp.int32, (CHUNK, CHUNK), 1)
        causal = li >= si
        tril = causal.astype(F32).astype(BF16)
        dt = _softplus(u.dt[...] + w.dtb[...])
        a = -jnp.exp(w.alog[...])
        hi, mid, lo = _split3(dt * a)
        a_cs = _dot(tril, hi) + _dot(tril, mid) + _dot(tril, lo)
        a_cs_t = a_cs.T[0:SSD_HEADS, :]
        dt_t = dt.T[0:SSD_HEADS, :]
        q_t = dt_t * jnp.exp(a_cs_t[:, CHUNK - 1:CHUNK] - a_cs_t)
        ctx.update(causal=causal, a_cs=a_cs, a_cs_t=a_cs_t, dt_t=dt_t, q_t=q_t)
    tasks.append(prep)

    def group_head(g):
        def run():
            bm = u.xc[:, SSD_INNER + g * SSD_STATE:SSD_INNER + (g + 1) * SSD_STATE]
            cm = u.xc[:, SSD_INNER + (SSD_GROUPS + g) * SSD_STATE:SSD_INNER + (SSD_GROUPS + g + 1) * SSD_STATE]
            cmb = cm.astype(BF16)
            ctx["cb"] = _dot_nt(cmb, bm.astype(BF16))
            ctx["bm_t"] = bm.T
            ctx["y_off"] = _dot(cmb, state_ref[g].astype(BF16))
            ctx["y"], ctx["st"], ctx["dec"] = [], [], []
        return run

    def pair(g, pr):
        def run():
            a_cs, a_cs_t, dt_t, q_t = ctx["a_cs"], ctx["a_cs_t"], ctx["dt_t"], ctx["q_t"]
            h0 = g * hpg + 2 * pr
            c0 = h0 * SSD_HEAD_DIM
            xpair = u.xc[:, c0:c0 + LANES]
            rhs = jnp.concatenate([jnp.where(lane_lo, xpair, 0.0),
                                   jnp.where(lane_lo, 0.0, xpair)], axis=0).astype(BF16)
            tops, bots, pbs = [], [], []
            for h in (h0, h0 + 1):
                col = jnp.broadcast_to(a_cs[:, h:h + 1], (CHUNK, CHUNK))
                lmat = jnp.exp(jnp.where(ctx["causal"], col - a_cs_t[h:h + 1, :], NEG))
                tops.append((ctx["cb"] * lmat * dt_t[h:h + 1, :]).astype(BF16))
                bots.append((ctx["bm_t"] * q_t[h:h + 1, :]).astype(BF16))
                pbs.append(jnp.exp(col))
            lhs = jnp.concatenate([jnp.concatenate(tops, axis=1), jnp.concatenate(bots, axis=1)], axis=0)
            r = _dot(lhs, rhs)
            pp = jnp.where(lane_lo, pbs[0], pbs[1])
            ctx["y"].append(r[0:CHUNK, :] + ctx["y_off"][:, pr * LANES:(pr + 1) * LANES] * pp
                            + xpair * w.dskip[:, c0:c0 + LANES])
            ctx["st"].append(r[CHUNK:2 * CHUNK, :])
            ctx["dec"].append(pp[CHUNK - 1:CHUNK, :])
        return run

    def group_tail(g):
        def run():
            state_ref[g] = (state_ref[g] * jnp.concatenate(ctx["dec"], axis=1)
                            + jnp.concatenate(ctx["st"], axis=1))
            yg = jnp.concatenate(ctx["y"], axis=1) * _silu(u.z[:, g * gw:(g + 1) * gw])
            ms = jnp.mean(yg * yg, axis=-1, keepdims=True)
            u.y[:, g * gw:(g + 1) * gw] = (
                yg * lax.rsqrt(ms + EPS) * w.ssdnw[:, g * gw:(g + 1) * gw]).astype(BF16)
        return run

    for g in range(SSD_GROUPS):
        tasks.append(group_head(g))
        for pr in range(hpg // 2):
            tasks.append(pair(g, pr))
        tasks.append(group_tail(g))
    return tasks


def _conformer_tasks(u, w):
    sh_rows = CHUNK + GLU_PAD - SUBLANES

    def shift(r):
        def run():
            u.sh[r - 1, :, :] = u.glu[r:r + sh_rows, :]
        return run

    def conv(rt):
        def run():
            base = rt * CONF_TILE
            acc = _rep_rows(w.cdb[...], CONF_TILE)
            for k in range(CONF_KERNEL):
                qo, ro = divmod(GLU_PAD - (CONF_KERNEL - 1) + k, SUBLANES)
                r0 = base + qo * SUBLANES
                src = u.glu[r0:r0 + CONF_TILE, :] if ro == 0 else u.sh[ro - 1, r0:r0 + CONF_TILE, :]
                acc = acc + _rep_rows(w.cdw[k], CONF_TILE) * src
            mu = jnp.mean(acc, axis=-1, keepdims=True)
            xc = acc - mu
            var = jnp.mean(xc * xc, axis=-1, keepdims=True)
            yl = xc * lax.rsqrt(var + EPS) * w.clnw[...] + w.clnb[...]
            u.y[base:base + CONF_TILE, SSD_INNER:SSD_INNER + CONF_WIDTH] = (
                _silu(yl) * _silu(u.cg[base:base + CONF_TILE, :])).astype(BF16)
        return run

    return [shift(r) for r in range(1, SUBLANES)] + [conv(rt) for rt in range(CHUNK // CONF_TILE)]


def _attention_tasks(u, w, bias_ref, first_pen):
    lane_lo = lax.broadcasted_iota(jnp.int32, (CHUNK, LANES), 1) < (LANES // 2)
    grp = ATTN_HEADS // ATTN_KV_HEADS

    def kv_head(kh):
        def run():
            kb = u.k[:, kh * LANES:(kh + 1) * LANES]
            vb = u.v[:, kh * LANES:(kh + 1) * LANES]
            qs = []
            for pr in range(grp // 2):
                c0 = (kh * grp + 2 * pr) * ATTN_HEAD_DIM
                qpair = u.q[:, c0:c0 + LANES].astype(F32)
                qs.append(jnp.where(lane_lo, qpair, 0.0).astype(BF16))
                qs.append(jnp.where(lane_lo, 0.0, qpair).astype(BF16))
            s = _dot_nt(jnp.concatenate(qs, axis=0), kb)
            es, inv = [], []
            for gi in range(grp):
                h = kh * grp + gi
                sh = s[gi * CHUNK:(gi + 1) * CHUNK, :] + bias_ref[h * CHUNK:(h + 1) * CHUNK, :]
                if first_pen is not None:
                    prev_half = lax.broadcasted_iota(jnp.int32, (CHUNK, 2 * CHUNK), 1) < CHUNK
                    sh = sh + jnp.where(prev_half, first_pen, 0.0)
                sink = w.sinks[h]
                m = jnp.maximum(jnp.max(sh, axis=-1, keepdims=True), sink)
                e = jnp.exp(sh - m)
                denom = jnp.sum(e, axis=-1, keepdims=True) + jnp.exp(sink - m)
                es.append(e.astype(BF16))
                inv.append(1.0 / denom)
            o = _dot(jnp.concatenate(es, axis=0), vb)
            outs = []
            for pr in range(grp // 2):
                o0 = o[(2 * pr) * CHUNK:(2 * pr + 1) * CHUNK, :] * inv[2 * pr]
                o1 = o[(2 * pr + 1) * CHUNK:(2 * pr + 2) * CHUNK, :] * inv[2 * pr + 1]
                outs.append(jnp.where(lane_lo, o0, o1))
            c0 = kh * grp * ATTN_HEAD_DIM
            gate = _silu(u.ag[:, c0:c0 + grp * ATTN_HEAD_DIM])
            u.y[:, SSD_INNER + CONF_WIDTH + c0:SSD_INNER + CONF_WIDTH + c0 + grp * ATTN_HEAD_DIM] = (
                jnp.concatenate(outs, axis=1) * gate).astype(BF16)
        return run

    return [kv_head(kh) for kh in range(ATTN_KV_HEADS)]


def _out_proj_tasks(u, w, get_x, get_p, out_ref, idx):
    def project():
        ov = _dot(u.y[...], w.wout[...])
        ms = jnp.mean(ov * ov, axis=-1, keepdims=True)
        out_ref[idx] = get_x() + ov * lax.rsqrt(ms + EPS) * w.postw[...]

    def embed():
        hv = out_ref[idx]
        gate = _sigmoid(_dot(hv.astype(BF16), w.wpg[...]))
        out_ref[idx] = hv + _dot(get_p().astype(BF16), w.wple[...]) * gate

    return [project, embed]


def _mixer_tasks(u, o, w, state_ref, bias_ref, first_pen):
    return ([lambda: _hand_over_history(u, o)] + _ssd_tasks(u, w, state_ref) + _conformer_tasks(u, w)
            + _attention_tasks(u, w, bias_ref, first_pen))


def _run_interleaved(main, side):
    done = 0
    for i, task in enumerate(main):
        task()
        while done < len(side) and (done + 1) * len(main) <= (i + 1) * len(side):
            side[done]()
            done += 1
    for task in side[done:]:
        task()


def _layer_kernel(steps_per_seq, *refs):
    (x_ref, xn_ref, p_ref, win, wout, wpg, wple, prew, convw, convb, dtb, alog, dskip, ssdnw,
     cdw, cdb, clnw, clnb, postw, bucket_ref, sinks, relb) = refs[:_N_IN]
    o_ref = refs[_N_IN]
    scratch = refs[_N_IN + 1:]
    n = len(_SET)
    ua = _Refs(**{name: r for (name, _, _), r in zip(_SET, scratch[:n])})
    ub = _Refs(**{name: r for (name, _, _), r in zip(_SET, scratch[n:2 * n])})
    state_ref, bias_ref = scratch[2 * n:]
    w = _Refs(win=win, wout=wout, wpg=wpg, wple=wple, prew=prew, convw=convw, convb=convb, dtb=dtb,
              alog=alog, dskip=dskip, ssdnw=ssdnw, cdw=cdw, cdb=cdb, clnw=clnw, clnb=clnb, postw=postw,
              sinks=sinks)

    step = pl.program_id(0)
    seq_start = lax.rem(step, steps_per_seq) == 0

    @pl.when(step == 0)
    def _():
        bucket = bucket_ref[...]
        qi = lax.broadcasted_iota(jnp.int32, (CHUNK, 2 * CHUNK), 0)
        si = lax.broadcasted_iota(jnp.int32, (CHUNK, 2 * CHUNK), 1)
        dist = qi + CHUNK - si
        in_window = (dist >= 0) & (dist < CHUNK)
        for h in range(ATTN_HEADS):
            acc = jnp.zeros((CHUNK, 2 * CHUNK), F32)
            for bkt in range(N_BUCKETS):
                acc = jnp.where(bucket == bkt, relb[bkt, h], acc)
            bias_ref[h * CHUNK:(h + 1) * CHUNK, :] = jnp.where(in_window, acc, NEG)
        for task in _in_proj_tasks(lambda: x_ref[0], ua, w):
            task()

    @pl.when(seq_start)
    def _():
        ua.xbc[0:XBC_PAD, :] = jnp.zeros((XBC_PAD, SSD_CONV_DIM), F32)
        ua.glu[0:GLU_PAD, :] = jnp.zeros((GLU_PAD, CONF_WIDTH), F32)
        ua.k[0:CHUNK, :] = jnp.zeros((CHUNK, 2 * LANES), BF16)
        ua.v[0:CHUNK, :] = jnp.zeros((CHUNK, 2 * LANES), BF16)
        state_ref[...] = jnp.zeros(state_ref.shape, F32)

    first_pen = jnp.where(seq_start, NEG, 0.0)

    _run_interleaved(_mixer_tasks(ua, ub, w, state_ref, bias_ref, first_pen),
                     _in_proj_tasks(lambda: x_ref[1], ub, w))
    _run_interleaved(_mixer_tasks(ub, ua, w, state_ref, bias_ref, None),
                     _out_proj_tasks(ua, w, lambda: x_ref[0], lambda: p_ref[0], o_ref, 0)
                     + _in_proj_tasks(lambda: xn_ref[...], ua, w))
    for task in _out_proj_tasks(ub, w, lambda: x_ref[1], lambda: p_ref[1], o_ref, 1):
        task()


def _reorder_w_in(w_in):
    sizes = (SSD_INNER, SSD_CONV_DIM, SSD_HEADS, 2 * CONF_WIDTH, CONF_WIDTH,
             ATTN_WIDTH, ATTN_KV_HEADS * ATTN_HEAD_DIM, ATTN_KV_HEADS * ATTN_HEAD_DIM, ATTN_WIDTH)
    parts, start = [], 0
    for sz in sizes:
        parts.append(w_in[:, start:start + sz])
        start += sz
    z, xbc, dt, cin, cg, q, k, v, ag = parts

    def dup(wm):
        heads = [wm[:, i * ATTN_HEAD_DIM:(i + 1) * ATTN_HEAD_DIM] for i in range(ATTN_KV_HEADS)]
        return jnp.concatenate([t for hd in heads for t in (hd, hd)], axis=1)

    dt_pad = jnp.pad(dt, ((0, 0), (0, LANES - SSD_HEADS)))
    return jnp.concatenate([z, xbc, cin, cg, q, dup(k), dup(v), ag, dt_pad], axis=1).astype(BF16)


def _row(v, width=None):
    v = v.reshape(1, -1).astype(F32)
    if width is not None and v.shape[1] < width:
        v = jnp.pad(v, ((0, 0), (0, width - v.shape[1])))
    return v


def _sublane_rep(v):
    v = v.astype(F32)
    return jnp.broadcast_to(v[..., None, :], v.shape[:-1] + (SUBLANES, v.shape[-1]))


def _full_spec(shape):
    return pl.BlockSpec(shape, lambda s: (0,) * len(shape))


def _weight_spec(shape):
    return pl.BlockSpec(shape, lambda s: (0,) * len(shape), pipeline_mode=pl.Buffered(1))


def _layer(x, p_i, w_in, w_out, w_pg, w_ple, pre_w, conv_w, conv_b, dt_bias, a_log, d_skip, ssd_nw,
           cdw, cdb, clnw, clnb, post_w, sinks, rel_bias, bucket):
    bsz, seq, _ = x.shape
    step_rows = CHUNKS_PER_STEP * CHUNK
    assert seq % step_rows == 0
    n_chunks = bsz * seq // CHUNK
    n_steps = n_chunks // CHUNKS_PER_STEP
    xc = x.reshape(n_chunks, CHUNK, D_MODEL)
    pc = p_i.reshape(n_chunks, CHUNK, PLE_DIM)

    small = [
        _row(pre_w), _sublane_rep(conv_w), _sublane_rep(conv_b), _row(dt_bias, LANES), _row(a_log, LANES),
        _row(jnp.repeat(d_skip, SSD_HEAD_DIM)), _row(ssd_nw),
        _sublane_rep(cdw), _sublane_rep(cdb), _row(clnw), _row(clnb),
        _row(post_w), bucket,
    ]
    weights = [_reorder_w_in(w_in), w_out.astype(BF16), w_pg.astype(BF16), w_ple.astype(BF16)]

    in_specs = (
        [pl.BlockSpec((CHUNKS_PER_STEP, CHUNK, D_MODEL), lambda s: (s, 0, 0)),
         pl.BlockSpec((None, CHUNK, D_MODEL),
                      lambda s: (jnp.minimum(CHUNKS_PER_STEP * (s + 1), n_chunks - 1), 0, 0)),
         pl.BlockSpec((CHUNKS_PER_STEP, CHUNK, PLE_DIM), lambda s: (s, 0, 0))]
        + [_weight_spec(wt.shape) for wt in weights]
        + [_full_spec(sm.shape) for sm in small]
        + [pl.BlockSpec(memory_space=pltpu.SMEM), pl.BlockSpec(memory_space=pltpu.SMEM)]
    )
    assert len(in_specs) == _N_IN
    scratch = ([pltpu.VMEM(shape, dtype) for _ in range(CHUNKS_PER_STEP) for (_, shape, dtype) in _SET]
               + [pltpu.VMEM((SSD_GROUPS, SSD_STATE, SSD_INNER // SSD_GROUPS), F32),
                  pltpu.VMEM((ATTN_HEADS * CHUNK, 2 * CHUNK), F32)])
    out = pl.pallas_call(
        functools.partial(_layer_kernel, seq // step_rows),
        out_shape=jax.ShapeDtypeStruct(xc.shape, x.dtype),
        grid=(n_steps,),
        in_specs=in_specs,
        out_specs=pl.BlockSpec((CHUNKS_PER_STEP, CHUNK, D_MODEL), lambda s: (s, 0, 0)),
        scratch_shapes=scratch,
        compiler_params=pltpu.CompilerParams(
            dimension_semantics=("arbitrary",),
            vmem_limit_bytes=VMEM_LIMIT_BYTES),
        name="hybrid_layer",
    )(xc, xc, pc, *weights, *small, sinks.astype(F32), rel_bias.astype(F32))
    return out.reshape(x.shape)


def kernel(x, p, pre_norm_w, w_in, ssd_conv_w, ssd_conv_b, ssd_dt_bias, ssd_a_log, ssd_d, ssd_norm_w,
           conf_dw_w, conf_dw_b, conf_ln_w, conf_ln_b, attn_sinks, rel_bias, w_out, post_norm_w,
           ple_proj, ple_gate):
    bucket = jnp.asarray(_t5_bucket_table())
    h = x
    for i in range(p.shape[0]):
        h = _layer(h, p[i], w_in[i], w_out[i], ple_gate[i], ple_proj[i], pre_norm_w[i], ssd_conv_w[i],
                   ssd_conv_b[i], ssd_dt_bias[i], ssd_a_log[i], ssd_d[i], ssd_norm_w[i], conf_dw_w[i],
                   conf_dw_b[i], conf_ln_w[i], conf_ln_b[i], post_norm_w[i], attn_sinks[i], rel_bias, bucket)
    return h


def _t5_bucket_table():
    q = np.arange(CHUNK)[:, None]
    s = np.arange(2 * CHUNK)[None, :]
    d = np.maximum(q + CHUNK - s, 0)
    max_exact = N_BUCKETS // 2
    large = max_exact + (np.log(np.maximum(d, 1).astype(np.float32) / max_exact)
                         / math.log(MAX_DISTANCE / max_exact) * (N_BUCKETS - max_exact)).astype(np.int32)
    large = np.minimum(large, N_BUCKETS - 1)
    return np.where(d < max_exact, d, large).astype(np.int32)
```

```python
import functools
import math

import numpy as np
import jax
import jax.numpy as jnp
from jax import lax
from jax.experimental import pallas as pl
from jax.experimental.pallas import tpu as pltpu

F32 = jnp.float32
BF16 = jnp.bfloat16

D_MODEL = 1024
SSD_INNER = 1024
SSD_HEADS = 16
SSD_HEAD_DIM = 64
SSD_GROUPS = 2
SSD_STATE = 128
SSD_CONV = 4
CHUNK = 128
SSD_CONV_DIM = SSD_INNER + 2 * SSD_GROUPS * SSD_STATE
CONF_WIDTH = 512
CONF_KERNEL = 31
ATTN_HEADS = 8
ATTN_KV_HEADS = 2
ATTN_HEAD_DIM = 64
ATTN_WIDTH = ATTN_HEADS * ATTN_HEAD_DIM
N_BUCKETS = 32
MAX_DISTANCE = 128
PLE_DIM = 256
MIX_WIDTH = SSD_INNER + CONF_WIDTH + ATTN_WIDTH
EPS = 1e-6
LOG2E = math.log2(math.e)
NEG = -1e30

LANES = 128
SUBLANES = 8
VMEM_LIMIT_BYTES = 56 * 1024 * 1024

CHUNKS_PER_STEP = 2
XBC_PAD = SUBLANES
GLU_PAD = 32
SSD_CONV_TILE = 64
CONF_TILE = 32

Z0 = 0
XBC0 = Z0 + SSD_INNER
CIN0 = XBC0 + SSD_CONV_DIM
CG0 = CIN0 + 2 * CONF_WIDTH
Q0 = CG0 + CONF_WIDTH
K0 = Q0 + ATTN_WIDTH
V0 = K0 + 2 * LANES
AG0 = V0 + 2 * LANES
DT0 = AG0 + ATTN_WIDTH
IN_COLS = DT0 + LANES

_SET = (
    ("hn", (CHUNK, D_MODEL), BF16),
    ("z", (CHUNK, SSD_INNER), F32),
    ("xbc", (SSD_CONV_DIM // LANES, XBC_PAD + CHUNK, LANES), F32),
    ("xc", (CHUNK, SSD_CONV_DIM), F32),
    ("dt", (CHUNK, LANES), F32),
    ("glu", (CONF_WIDTH // LANES, GLU_PAD + CHUNK, LANES), F32),
    ("cg", (CHUNK, CONF_WIDTH), F32),
    ("q", (CHUNK, ATTN_WIDTH), BF16),
    ("k", (2 * CHUNK, 2 * LANES), BF16),
    ("v", (2 * CHUNK, 2 * LANES), BF16),
    ("ag", (CHUNK, ATTN_WIDTH), F32),
    ("y", (CHUNK, MIX_WIDTH), BF16),
)
_N_IN = 22


def _sigmoid(x):
    return 1.0 / (1.0 + jnp.exp(-x))


def _silu(x):
    half = 0.5 * x
    return half + half * jnp.tanh(half)


def _softplus(x):
    return jnp.maximum(x, 0.0) + jnp.log(1.0 + jnp.exp(-jnp.abs(x)))


def _split3(x):
    hi = x.astype(BF16)
    r1 = x - hi.astype(F32)
    mid = r1.astype(BF16)
    lo = (r1 - mid.astype(F32)).astype(BF16)
    return hi, mid, lo


def _dot(a, b):
    return jnp.dot(a, b, preferred_element_type=F32)


def _dot_nt(a, b):
    return lax.dot_general(a, b, (((1,), (1,)), ((), ())), preferred_element_type=F32)


class _Refs:
    def __init__(self, **kw):
        self.__dict__.update(kw)


def _in_proj_tasks(get_x, u, w):
    def prenorm():
        xv = get_x()
        ms = jnp.mean(xv * xv, axis=-1, keepdims=True)
        u.hn[...] = (xv * lax.rsqrt(ms + EPS) * w.prew[...]).astype(BF16)

    def proj(c0, width):
        return _dot(u.hn[...], w.win[:, c0:c0 + width])

    def glu(half):
        def run():
            c0 = half * (CONF_WIDTH // 2)
            av = proj(CIN0 + c0, CONF_WIDTH // 2)
            gv = proj(CIN0 + CONF_WIDTH + c0, CONF_WIDTH // 2)
            store_slabs(u.glu, GLU_PAD, c0)(av * _sigmoid(gv))
        return run

    def seg(c0, width, store):
        return lambda: store(proj(c0, width))

    def store_slabs(ref, r0, c0):
        def store(val):
            for t in range(val.shape[1] // LANES):
                ref[c0 // LANES + t, r0:r0 + CHUNK, :] = val[:, t * LANES:(t + 1) * LANES]
        return store

    def store_to(ref, r0, c0, width, cast=None, scale=None):
        def store(val):
            if scale is not None:
                val = val * scale
            ref[r0:r0 + CHUNK, c0:c0 + width] = val if cast is None else val.astype(cast)
        return store

    seg_w = 512
    tasks = [prenorm, glu(0), glu(1), seg(CG0, CONF_WIDTH, store_to(u.cg, 0, 0, CONF_WIDTH))]
    for c0 in range(0, SSD_CONV_DIM, seg_w):
        tasks.append(seg(XBC0 + c0, seg_w, store_slabs(u.xbc, XBC_PAD, c0)))
    tasks.append(seg(DT0, LANES, store_to(u.dt, 0, 0, LANES)))
    for c0 in range(0, SSD_INNER, seg_w):
        tasks.append(seg(Z0 + c0, seg_w, store_to(u.z, 0, c0, seg_w)))
    tasks.append(seg(Q0, ATTN_WIDTH, store_to(u.q, 0, 0, ATTN_WIDTH, cast=BF16, scale=ATTN_HEAD_DIM ** -0.5)))
    tasks.append(seg(AG0, ATTN_WIDTH, store_to(u.ag, 0, 0, ATTN_WIDTH)))
    tasks.append(seg(K0, 2 * LANES, store_to(u.k, CHUNK, 0, 2 * LANES, cast=BF16)))
    tasks.append(seg(V0, 2 * LANES, store_to(u.v, CHUNK, 0, 2 * LANES, cast=BF16)))
    return tasks


def _hand_over_history(u, o):
    o.xbc[:, 0:XBC_PAD, :] = u.xbc[:, CHUNK:CHUNK + XBC_PAD, :]
    o.glu[:, 0:GLU_PAD, :] = u.glu[:, CHUNK:CHUNK + GLU_PAD, :]
    o.k[0:CHUNK, :] = u.k[CHUNK:2 * CHUNK, :]
    o.v[0:CHUNK, :] = u.v[CHUNK:2 * CHUNK, :]


def _rep_rows(v, rows):
    return jnp.concatenate([v] * (rows // SUBLANES), axis=0)


def _ssd_tasks(u, w, state_ref):
    tasks = []
    ctx = {}
    lane_lo = lax.broadcasted_iota(jnp.int32, (CHUNK, LANES), 1) < (LANES // 2)
    hpg = SSD_HEADS // SSD_GROUPS
    gw = SSD_INNER // SSD_GROUPS
    conv_cols = 512

    def conv(rt, ct):
        def run():
            for t in range(ct * conv_cols // LANES, (ct + 1) * conv_cols // LANES):
                cols = slice(t * LANES, (t + 1) * LANES)
                acc = _rep_rows(w.convb[:, cols], SSD_CONV_TILE)
                for k in range(SSD_CONV):
                    r0 = XBC_PAD + rt * SSD_CONV_TILE - (SSD_CONV - 1) + k
                    acc = acc + _rep_rows(w.convw[k, :, cols], SSD_CONV_TILE) * u.xbc[t, r0:r0 + SSD_CONV_TILE, :]
                u.xc[rt * SSD_CONV_TILE:(rt + 1) * SSD_CONV_TILE, cols] = _silu(acc)
        return run
    for rt in range(CHUNK // SSD_CONV_TILE):
        for ct in range(SSD_CONV_DIM // conv_cols):
            tasks.append(conv(rt, ct))

    def prep():
        li = lax.broadcasted_iota(jnp.int32, (CHUNK, CHUNK), 0)
        si = lax.broadcasted_iota(jnp.int32, (CHUNK, CHUNK), 1)
        causal = li >= si
        triu = (li <= si).astype(F32).astype(BF16)
        dt_t = _softplus(u.dt[...].T[0:SSD_HEADS, :] + w.dtb[...])
        a = -jnp.exp(w.alog[...])
        hi, mid, lo = _split3(dt_t * a)
        a_cs_t = _dot(hi, triu) + _dot(mid, triu) + _dot(lo, triu)
        a2_t = a_cs_t * LOG2E
        a2 = jnp.concatenate([a2_t, jnp.zeros((CHUNK - SSD_HEADS, CHUNK), F32)], axis=0).T
        q_t = dt_t * jnp.exp2(a2_t[:, CHUNK - 1:CHUNK] - a2_t)
        row_t = a2_t - jnp.log2(dt_t)
        ctx.update(causal=causal, a2=a2, row_t=row_t, q_t=q_t)
    tasks.append(prep)

    def group_head(g):
        def run():
            bm = u.xc[:, SSD_INNER + g * SSD_STATE:SSD_INNER + (g + 1) * SSD_STATE]
            cm = u.xc[:, SSD_INNER + (SSD_GROUPS + g) * SSD_STATE:SSD_INNER + (SSD_GROUPS + g + 1) * SSD_STATE]
            cmb = cm.astype(BF16)
            ctx["cb"] = _dot_nt(cmb, bm.astype(BF16))
            ctx["bm_t"] = bm.T
            ctx["y_off"] = _dot(cmb, state_ref[g].astype(BF16))
            ctx["y"], ctx["st"], ctx["dec"] = [], [], []
        return run

    def pair(g, pr):
        def run():
            a2, row_t, q_t = ctx["a2"], ctx["row_t"], ctx["q_t"]
            h0 = g * hpg + 2 * pr
            c0 = h0 * SSD_HEAD_DIM
            xpair = u.xc[:, c0:c0 + LANES]
            rhs = jnp.concatenate([jnp.where(lane_lo, xpair, 0.0),
                                   jnp.where(lane_lo, 0.0, xpair)], axis=0).astype(BF16)
            tops, bots, pbs = [], [], []
            for h in (h0, h0 + 1):
                col = jnp.broadcast_to(a2[:, h:h + 1], (CHUNK, CHUNK))
                lmat = jnp.exp2(jnp.where(ctx["causal"], col - row_t[h:h + 1, :], NEG))
                tops.append((ctx["cb"] * lmat).astype(BF16))
                bots.append((ctx["bm_t"] * q_t[h:h + 1, :]).astype(BF16))
                pbs.append(jnp.exp2(col))
            lhs = jnp.concatenate([jnp.concatenate(tops, axis=1), jnp.concatenate(bots, axis=1)], axis=0)
            r = _dot(lhs, rhs)
            pp = jnp.where(lane_lo, pbs[0], pbs[1])
            ctx["y"].append(r[0:CHUNK, :] + ctx["y_off"][:, pr * LANES:(pr + 1) * LANES] * pp
                            + xpair * w.dskip[:, c0:c0 + LANES])
            ctx["st"].append(r[CHUNK:2 * CHUNK, :])
            ctx["dec"].append(pp[CHUNK - 1:CHUNK, :])
        return run

    def group_tail(g):
        def run():
            state_ref[g] = (state_ref[g] * jnp.concatenate(ctx["dec"], axis=1)
                            + jnp.concatenate(ctx["st"], axis=1))
            yg = jnp.concatenate(ctx["y"], axis=1) * _silu(u.z[:, g * gw:(g + 1) * gw])
            ms = jnp.mean(yg * yg, axis=-1, keepdims=True)
            u.y[:, g * gw:(g + 1) * gw] = (
                yg * lax.rsqrt(ms + EPS) * w.ssdnw[:, g * gw:(g + 1) * gw]).astype(BF16)
        return run

    for g in range(SSD_GROUPS):
        tasks.append(group_head(g))
        for pr in range(hpg // 2):
            tasks.append(pair(g, pr))
        tasks.append(group_tail(g))
    return tasks


def _conformer_tasks(u, w):
    def conv(rt):
        def run():
            base = rt * CONF_TILE
            accs = []
            for t in range(CONF_WIDTH // LANES):
                cols = slice(t * LANES, (t + 1) * LANES)
                acc = _rep_rows(w.cdb[:, cols], CONF_TILE)
                for k in range(CONF_KERNEL):
                    r0 = base + GLU_PAD - (CONF_KERNEL - 1) + k
                    acc = acc + _rep_rows(w.cdw[k, :, cols], CONF_TILE) * u.glu[t, r0:r0 + CONF_TILE, :]
                accs.append(acc)
            acc = jnp.concatenate(accs, axis=1)
            mu = jnp.mean(acc, axis=-1, keepdims=True)
            xc = acc - mu
            var = jnp.mean(xc * xc, axis=-1, keepdims=True)
            yl = xc * lax.rsqrt(var + EPS) * w.clnw[...] + w.clnb[...]
            u.y[base:base + CONF_TILE, SSD_INNER:SSD_INNER + CONF_WIDTH] = (
                _silu(yl) * _silu(u.cg[base:base + CONF_TILE, :])).astype(BF16)
        return run

    return [conv(rt) for rt in range(CHUNK // CONF_TILE)]


def _attention_tasks(u, w, bias_ref, first_pen):
    lane_lo = lax.broadcasted_iota(jnp.int32, (CHUNK, LANES), 1) < (LANES // 2)
    grp = ATTN_HEADS // ATTN_KV_HEADS

    def kv_head(kh):
        def run():
            kb = u.k[:, kh * LANES:(kh + 1) * LANES]
            vb = u.v[:, kh * LANES:(kh + 1) * LANES]
            qs = []
            for pr in range(grp // 2):
                c0 = (kh * grp + 2 * pr) * ATTN_HEAD_DIM
                qpair = u.q[:, c0:c0 + LANES].astype(F32)
                qs.append(jnp.where(lane_lo, qpair, 0.0).astype(BF16))
                qs.append(jnp.where(lane_lo, 0.0, qpair).astype(BF16))
            s = _dot_nt(jnp.concatenate(qs, axis=0), kb)
            es, inv = [], []
            for gi in range(grp):
                h = kh * grp + gi
                sh = s[gi * CHUNK:(gi + 1) * CHUNK, :] + bias_ref[h * CHUNK:(h + 1) * CHUNK, :]
                if first_pen is not None:
                    prev_half = lax.broadcasted_iota(jnp.int32, (CHUNK, 2 * CHUNK), 1) < CHUNK
                    sh = sh + jnp.where(prev_half, first_pen, 0.0)
                sink = w.sinks[h]
                m = jnp.maximum(jnp.max(sh, axis=-1, keepdims=True), sink)
                e = jnp.exp(sh - m)
                denom = jnp.sum(e, axis=-1, keepdims=True) + jnp.exp(sink - m)
                es.append(e.astype(BF16))
                inv.append(1.0 / denom)
            o = _dot(jnp.concatenate(es, axis=0), vb)
            outs = []
            for pr in range(grp // 2):
                o0 = o[(2 * pr) * CHUNK:(2 * pr + 1) * CHUNK, :] * inv[2 * pr]
                o1 = o[(2 * pr + 1) * CHUNK:(2 * pr + 2) * CHUNK, :] * inv[2 * pr + 1]
                outs.append(jnp.where(lane_lo, o0, o1))
            c0 = kh * grp * ATTN_HEAD_DIM
            gate = _silu(u.ag[:, c0:c0 + grp * ATTN_HEAD_DIM])
            u.y[:, SSD_INNER + CONF_WIDTH + c0:SSD_INNER + CONF_WIDTH + c0 + grp * ATTN_HEAD_DIM] = (
                jnp.concatenate(outs, axis=1) * gate).astype(BF16)
        return run

    return [kv_head(kh) for kh in range(ATTN_KV_HEADS)]


def _out_proj_tasks(u, w, get_x, get_p, out_ref, idx):
    def project():
        ov = _dot(u.y[...], w.wout[:, 0:D_MODEL])
        ms = jnp.mean(ov * ov, axis=-1, keepdims=True)
        out_ref[idx] = get_x() + ov * lax.rsqrt(ms + EPS) * w.postw[...]

    def embed():
        hv = out_ref[idx]
        gate = _sigmoid(_dot(hv.astype(BF16), w.wpg[:, 0:D_MODEL]))
        out_ref[idx] = hv + _dot(get_p().astype(BF16), w.wple[:, 0:D_MODEL]) * gate

    return [project, embed]


def _mixer_tasks(u, o, w, state_ref, bias_ref, first_pen):
    return ([lambda: _hand_over_history(u, o)] + _ssd_tasks(u, w, state_ref) + _conformer_tasks(u, w)
            + _attention_tasks(u, w, bias_ref, first_pen))


def _run_interleaved(main, side):
    done = 0
    for i, task in enumerate(main):
        task()
        while done < len(side) and (done + 1) * len(main) <= (i + 1) * len(side):
            side[done]()
            done += 1
    for task in side[done:]:
        task()


def _layer_kernel(steps_per_seq, *refs):
    (x_ref, xn_ref, p_ref, win, wout, wpg, wple, prew, convw, convb, dtb, alog, dskip, ssdnw,
     cdw, cdb, clnw, clnb, postw, bucket_ref, sinks, relb) = refs[:_N_IN]
    o_ref = refs[_N_IN]
    scratch = refs[_N_IN + 1:]
    n = len(_SET)
    ua = _Refs(**{name: r for (name, _, _), r in zip(_SET, scratch[:n])})
    ub = _Refs(**{name: r for (name, _, _), r in zip(_SET, scratch[n:2 * n])})
    state_ref, bias_ref = scratch[2 * n:]
    w = _Refs(win=win, wout=wout, wpg=wpg, wple=wple, prew=prew, convw=convw, convb=convb, dtb=dtb,
              alog=alog, dskip=dskip, ssdnw=ssdnw, cdw=cdw, cdb=cdb, clnw=clnw, clnb=clnb, postw=postw,
              sinks=sinks)

    step = pl.program_id(0)
    seq_start = lax.rem(step, steps_per_seq) == 0

    @pl.when(step == 0)
    def _():
        bucket = bucket_ref[...]
        qi = lax.broadcasted_iota(jnp.int32, (CHUNK, 2 * CHUNK), 0)
        si = lax.broadcasted_iota(jnp.int32, (CHUNK, 2 * CHUNK), 1)
        dist = qi + CHUNK - si
        in_window = (dist >= 0) & (dist < CHUNK)
        for h in range(ATTN_HEADS):
            acc = jnp.zeros((CHUNK, 2 * CHUNK), F32)
            for bkt in range(N_BUCKETS):
                acc = jnp.where(bucket == bkt, relb[bkt, h], acc)
            bias_ref[h * CHUNK:(h + 1) * CHUNK, :] = jnp.where(in_window, acc, NEG)
        for task in _in_proj_tasks(lambda: x_ref[0], ua, w):
            task()

    @pl.when(seq_start)
    def _():
        ua.xbc[:, 0:XBC_PAD, :] = jnp.zeros((SSD_CONV_DIM // LANES, XBC_PAD, LANES), F32)
        ua.glu[:, 0:GLU_PAD, :] = jnp.zeros((CONF_WIDTH // LANES, GLU_PAD, LANES), F32)
        ua.k[0:CHUNK, :] = jnp.zeros((CHUNK, 2 * LANES), BF16)
        ua.v[0:CHUNK, :] = jnp.zeros((CHUNK, 2 * LANES), BF16)
        state_ref[...] = jnp.zeros(state_ref.shape, F32)

    first_pen = jnp.where(seq_start, NEG, 0.0)

    _run_interleaved(_mixer_tasks(ua, ub, w, state_ref, bias_ref, first_pen),
                     _in_proj_tasks(lambda: x_ref[1], ub, w))
    _run_interleaved(_mixer_tasks(ub, ua, w, state_ref, bias_ref, None),
                     _out_proj_tasks(ua, w, lambda: x_ref[0], lambda: p_ref[0], o_ref, 0)
                     + _in_proj_tasks(lambda: xn_ref[...], ua, w))
    for task in _out_proj_tasks(ub, w, lambda: x_ref[1], lambda: p_ref[1], o_ref, 1):
        task()


def _reorder_w_in(w_in):
    sizes = (SSD_INNER, SSD_CONV_DIM, SSD_HEADS, 2 * CONF_WIDTH, CONF_WIDTH,
             ATTN_WIDTH, ATTN_KV_HEADS * ATTN_HEAD_DIM, ATTN_KV_HEADS * ATTN_HEAD_DIM, ATTN_WIDTH)
    parts, start = [], 0
    for sz in sizes:
        parts.append(w_in[:, start:start + sz])
        start += sz
    z, xbc, dt, cin, cg, q, k, v, ag = parts

    def dup(wm):
        heads = [wm[:, i * ATTN_HEAD_DIM:(i + 1) * ATTN_HEAD_DIM] for i in range(ATTN_KV_HEADS)]
        return jnp.concatenate([t for hd in heads for t in (hd, hd)], axis=1)

    dt_pad = jnp.pad(dt, ((0, 0), (0, LANES - SSD_HEADS)))
    return jnp.concatenate([z, xbc, cin, cg, q, dup(k), dup(v), ag, dt_pad], axis=1).astype(BF16)


def _pad_pitch(wm):
    return jnp.pad(wm.astype(BF16), ((0, 0), (0, LANES)))


def _col(v):
    return v.reshape(-1, 1).astype(F32)


def _row(v, width=None):
    v = v.reshape(1, -1).astype(F32)
    if width is not None and v.shape[1] < width:
        v = jnp.pad(v, ((0, 0), (0, width - v.shape[1])))
    return v


def _sublane_rep(v):
    v = v.astype(F32)
    return jnp.broadcast_to(v[..., None, :], v.shape[:-1] + (SUBLANES, v.shape[-1]))


def _full_spec(shape):
    return pl.BlockSpec(shape, lambda s: (0,) * len(shape))


def _weight_spec(shape):
    return pl.BlockSpec(shape, lambda s: (0,) * len(shape), pipeline_mode=pl.Buffered(1))


def _layer(x, p_i, w_in, w_out, w_pg, w_ple, pre_w, conv_w, conv_b, dt_bias, a_log, d_skip, ssd_nw,
           cdw, cdb, clnw, clnb, post_w, sinks, rel_bias, bucket):
    bsz, seq, _ = x.shape
    step_rows = CHUNKS_PER_STEP * CHUNK
    assert seq % step_rows == 0
    n_chunks = bsz * seq // CHUNK
    n_steps = n_chunks // CHUNKS_PER_STEP
    xc = x.reshape(n_chunks, CHUNK, D_MODEL)
    pc = p_i.reshape(n_chunks, CHUNK, PLE_DIM)

    small = [
        _row(pre_w), _sublane_rep(conv_w), _sublane_rep(conv_b), _col(dt_bias), _col(a_log),
        _row(jnp.repeat(d_skip, SSD_HEAD_DIM)), _row(ssd_nw),
        _sublane_rep(cdw), _sublane_rep(cdb), _row(clnw), _row(clnb),
        _row(post_w), bucket,
    ]
    weights = [_reorder_w_in(w_in), _pad_pitch(w_out), _pad_pitch(w_pg), _pad_pitch(w_ple)]

    in_specs = (
        [pl.BlockSpec((CHUNKS_PER_STEP, CHUNK, D_MODEL), lambda s: (s, 0, 0)),
         pl.BlockSpec((None, CHUNK, D_MODEL),
                      lambda s: (jnp.minimum(CHUNKS_PER_STEP * (s + 1), n_chunks - 1), 0, 0)),
         pl.BlockSpec((CHUNKS_PER_STEP, CHUNK, PLE_DIM), lambda s: (s, 0, 0))]
        + [_weight_spec(wt.shape) for wt in weights]
        + [_full_spec(sm.shape) for sm in small]
        + [pl.BlockSpec(memory_space=pltpu.SMEM), pl.BlockSpec(memory_space=pltpu.SMEM)]
    )
    assert len(in_specs) == _N_IN
    scratch = ([pltpu.VMEM(shape, dtype) for _ in range(CHUNKS_PER_STEP) for (_, shape, dtype) in _SET]
               + [pltpu.VMEM((SSD_GROUPS, SSD_STATE, SSD_INNER // SSD_GROUPS), F32),
                  pltpu.VMEM((ATTN_HEADS * CHUNK, 2 * CHUNK), F32)])
    out = pl.pallas_call(
        functools.partial(_layer_kernel, seq // step_rows),
        out_shape=jax.ShapeDtypeStruct(xc.shape, x.dtype),
        grid=(n_steps,),
        in_specs=in_specs,
        out_specs=pl.BlockSpec((CHUNKS_PER_STEP, CHUNK, D_MODEL), lambda s: (s, 0, 0)),
        scratch_shapes=scratch,
        compiler_params=pltpu.CompilerParams(
            dimension_semantics=("arbitrary",),
            vmem_limit_bytes=VMEM_LIMIT_BYTES),
        name="hybrid_layer",
    )(xc, xc, pc, *weights, *small, sinks.astype(F32), rel_bias.astype(F32))
    return out.reshape(x.shape)


def kernel(x, p, pre_norm_w, w_in, ssd_conv_w, ssd_conv_b, ssd_dt_bias, ssd_a_log, ssd_d, ssd_norm_w,
           conf_dw_w, conf_dw_b, conf_ln_w, conf_ln_b, attn_sinks, rel_bias, w_out, post_norm_w,
           ple_proj, ple_gate):
    bucket = jnp.asarray(_t5_bucket_table())
    h = x
    for i in range(p.shape[0]):
        h = _layer(h, p[i], w_in[i], w_out[i], ple_gate[i], ple_proj[i], pre_norm_w[i], ssd_conv_w[i],
                   ssd_conv_b[i], ssd_dt_bias[i], ssd_a_log[i], ssd_d[i], ssd_norm_w[i], conf_dw_w[i],
                   conf_dw_b[i], conf_ln_w[i], conf_ln_b[i], post_norm_w[i], attn_sinks[i], rel_bias, bucket)
    return h


def _t5_bucket_table():
    q = np.arange(CHUNK)[:, None]
    s = np.arange(2 * CHUNK)[None, :]
    d = np.maximum(q + CHUNK - s, 0)
    max_exact = N_BUCKETS // 2
    large = max_exact + (np.log(np.maximum(d, 1).astype(np.float32) / max_exact)
                         / math.log(MAX_DISTANCE / max_exact) * (N_BUCKETS - max_exact)).astype(np.int32)
    large = np.minimum(large, N_BUCKETS - 1)
    return np.where(d < max_exact, d, large).astype(np.int32)
```

```python
import functools
import math

import numpy as np
import jax
import jax.numpy as jnp
from jax import lax
from jax.experimental import pallas as pl
from jax.experimental.pallas import tpu as pltpu

F32 = jnp.float32
BF16 = jnp.bfloat16

D_MODEL = 1024
SSD_INNER = 1024
SSD_HEADS = 16
SSD_HEAD_DIM = 64
SSD_GROUPS = 2
SSD_STATE = 128
SSD_CONV = 4
CHUNK = 128
SSD_CONV_DIM = SSD_INNER + 2 * SSD_GROUPS * SSD_STATE
CONF_WIDTH = 512
CONF_KERNEL = 31
ATTN_HEADS = 8
ATTN_KV_HEADS = 2
ATTN_HEAD_DIM = 64
ATTN_WIDTH = ATTN_HEADS * ATTN_HEAD_DIM
N_BUCKETS = 32
MAX_DISTANCE = 128
PLE_DIM = 256
MIX_WIDTH = SSD_INNER + CONF_WIDTH + ATTN_WIDTH
EPS = 1e-6
LOG2E = math.log2(math.e)
NEG = -1e30

LANES = 128
SUBLANES = 8
VMEM_LIMIT_BYTES = 56 * 1024 * 1024

CHUNKS_PER_STEP = 2
XBC_PAD = SUBLANES
GLU_PAD = 32
CONF_TILE = 32
PROJ_SEG = 256

COST_SSD_CONV = 50
COST_SSD_PREP = 40
COST_SSD_PAIR = 100
COST_SSD_GATE = 280
COST_CONF_CONV = 70
COST_CONF_NORM = 60
COST_ATTN_QK = 30
COST_ATTN_SOFTMAX = 70
COST_ATTN_PV = 60

A_COLS = SSD_INNER + SSD_CONV_DIM
B_START = A_COLS + SSD_HEADS
KV_WIDTH = ATTN_KV_HEADS * ATTN_HEAD_DIM
B_COLS = 3 * CONF_WIDTH + 2 * ATTN_WIDTH + 2 * KV_WIDTH
Z0 = ("a", 0)
XBC0 = ("a", SSD_INNER)
CIN0 = ("b", 0)
CG0 = ("b", 2 * CONF_WIDTH)
Q0 = ("b", 3 * CONF_WIDTH)
AG0 = ("b", 3 * CONF_WIDTH + ATTN_WIDTH + 2 * KV_WIDTH)
K0 = ("c", 0)
V0 = ("c", 2 * LANES)
DT0 = ("c", 4 * LANES)
C_COLS = 5 * LANES

_SET = (
    ("hn", (CHUNK, D_MODEL), BF16),
    ("z", (CHUNK, SSD_INNER), F32),
    ("xbc", (SSD_CONV_DIM // LANES, XBC_PAD + CHUNK, LANES), F32),
    ("xc", (CHUNK, SSD_CONV_DIM), F32),
    ("dt", (CHUNK, LANES), F32),
    ("glu", (CONF_WIDTH // LANES, GLU_PAD + CHUNK, LANES), F32),
    ("cg", (CHUNK, CONF_WIDTH), F32),
    ("q", (CHUNK, ATTN_WIDTH), BF16),
    ("k", (2 * CHUNK, 2 * LANES), BF16),
    ("v", (2 * CHUNK, 2 * LANES), BF16),
    ("ag", (CHUNK, ATTN_WIDTH), F32),
    ("y", (CHUNK, MIX_WIDTH), BF16),
)
_N_IN = 24


def _sigmoid(x):
    return 1.0 / (1.0 + jnp.exp(-x))


def _silu(x):
    half = 0.5 * x
    return half + half * jnp.tanh(half)


def _softplus(x):
    return jnp.maximum(x, 0.0) + jnp.log(1.0 + jnp.exp(-jnp.abs(x)))


def _split3(x):
    hi = x.astype(BF16)
    r1 = x - hi.astype(F32)
    mid = r1.astype(BF16)
    lo = (r1 - mid.astype(F32)).astype(BF16)
    return hi, mid, lo


def _dot(a, b):
    return jnp.dot(a, b, preferred_element_type=F32)


def _dot_nt(a, b):
    return lax.dot_general(a, b, (((1,), (1,)), ((), ())), preferred_element_type=F32)


class _Refs:
    def __init__(self, **kw):
        self.__dict__.update(kw)


def _in_proj_tasks(get_x, u, w):
    def prenorm():
        xv = get_x()
        ms = jnp.mean(xv * xv, axis=-1, keepdims=True)
        u.hn[...] = (xv * lax.rsqrt(ms + EPS) * w.prew[...]).astype(BF16)

    def proj(seg0, width, off=0):
        piece, c0 = seg0
        return _dot(u.hn[...], w.win[piece][:, c0 + off:c0 + off + width])

    def glu(half):
        def run():
            c0 = half * (CONF_WIDTH // 2)
            av = proj(CIN0, CONF_WIDTH // 2, c0)
            gv = proj(CIN0, CONF_WIDTH // 2, CONF_WIDTH + c0)
            store_slabs(u.glu, GLU_PAD, c0)(av * _sigmoid(gv))
        return (CONF_WIDTH, run)

    def seg(seg0, width, store, off=0):
        return (width, lambda: store(proj(seg0, width, off)))

    def store_slabs(ref, r0, c0):
        def store(val):
            for t in range(val.shape[1] // LANES):
                ref[c0 // LANES + t, r0:r0 + CHUNK, :] = val[:, t * LANES:(t + 1) * LANES]
        return store

    def store_to(ref, r0, c0, width, cast=None, scale=None):
        def store(val):
            if scale is not None:
                val = val * scale
            ref[r0:r0 + CHUNK, c0:c0 + width] = val if cast is None else val.astype(cast)
        return store

    seg_w = PROJ_SEG
    tasks = [(LANES, prenorm), glu(0), glu(1)]
    for c0 in range(0, CONF_WIDTH, seg_w):
        tasks.append(seg(CG0, seg_w, store_to(u.cg, 0, c0, seg_w), c0))
    for c0 in range(0, SSD_CONV_DIM, seg_w):
        tasks.append(seg(XBC0, seg_w, store_slabs(u.xbc, XBC_PAD, c0), c0))
    tasks.append(seg(DT0, LANES, store_to(u.dt, 0, 0, LANES)))
    for c0 in range(0, SSD_INNER, seg_w):
        tasks.append(seg(Z0, seg_w, store_to(u.z, 0, c0, seg_w), c0))
    for c0 in range(0, ATTN_WIDTH, seg_w):
        tasks.append(seg(Q0, seg_w, store_to(u.q, 0, c0, seg_w, cast=BF16, scale=ATTN_HEAD_DIM ** -0.5), c0))
        tasks.append(seg(AG0, seg_w, store_to(u.ag, 0, c0, seg_w), c0))
    tasks.append(seg(K0, 2 * LANES, store_to(u.k, CHUNK, 0, 2 * LANES, cast=BF16)))
    tasks.append(seg(V0, 2 * LANES, store_to(u.v, CHUNK, 0, 2 * LANES, cast=BF16)))
    return tasks


def _hand_over_history(u, o):
    o.xbc[:, 0:XBC_PAD, :] = u.xbc[:, CHUNK:CHUNK + XBC_PAD, :]
    o.glu[:, 0:GLU_PAD, :] = u.glu[:, CHUNK:CHUNK + GLU_PAD, :]
    o.k[0:CHUNK, :] = u.k[CHUNK:2 * CHUNK, :]
    o.v[0:CHUNK, :] = u.v[CHUNK:2 * CHUNK, :]


def _rep_rows(v, rows):
    return jnp.concatenate([v] * (rows // SUBLANES), axis=0)


def _ssd_tasks(u, w, state_ref):
    tasks = []
    ctx = {}
    lane_lo = lax.broadcasted_iota(jnp.int32, (CHUNK, LANES), 1) < (LANES // 2)
    hpg = SSD_HEADS // SSD_GROUPS
    gw = SSD_INNER // SSD_GROUPS

    def conv(t):
        def run():
            cols = slice(t * LANES, (t + 1) * LANES)
            acc = _rep_rows(w.convb[:, cols], CHUNK)
            for k in range(SSD_CONV):
                r0 = XBC_PAD - (SSD_CONV - 1) + k
                acc = acc + _rep_rows(w.convw[k, :, cols], CHUNK) * u.xbc[t, r0:r0 + CHUNK, :]
            u.xc[:, cols] = _silu(acc)
        return run
    for t in range(SSD_CONV_DIM // LANES):
        tasks.append((COST_SSD_CONV, conv(t)))

    def prep():
        li = lax.broadcasted_iota(jnp.int32, (CHUNK, CHUNK), 0)
        si = lax.broadcasted_iota(jnp.int32, (CHUNK, CHUNK), 1)
        causal = li >= si
        triu = (li <= si).astype(F32).astype(BF16)
        dt_t = _softplus(u.dt[...].T[0:SSD_HEADS, :] + w.dtb[...])
        a = -jnp.exp(w.alog[...])
        hi, mid, lo = _split3(dt_t * a)
        a_cs_t = _dot(hi, triu) + _dot(mid, triu) + _dot(lo, triu)
        a2_t = a_cs_t * LOG2E
        a2 = jnp.concatenate([a2_t, jnp.zeros((CHUNK - SSD_HEADS, CHUNK), F32)], axis=0).T
        q_t = dt_t * jnp.exp2(a2_t[:, CHUNK - 1:CHUNK] - a2_t)
        row_t = a2_t - jnp.log2(dt_t)
        ctx.update(causal=causal, a2=a2, row_t=row_t, q_t=q_t)
    tasks.append((COST_SSD_PREP, prep))

    def group_head(g):
        def run():
            bm = u.xc[:, SSD_INNER + g * SSD_STATE:SSD_INNER + (g + 1) * SSD_STATE]
            cm = u.xc[:, SSD_INNER + (SSD_GROUPS + g) * SSD_STATE:SSD_INNER + (SSD_GROUPS + g + 1) * SSD_STATE]
            cmb = cm.astype(BF16)
            ctx["cb"] = _dot_nt(cmb, bm.astype(BF16))
            ctx["bm_t"] = bm.T
            ctx["y_off"] = _dot(cmb, state_ref[g].astype(BF16))
            ctx["y"], ctx["st"], ctx["dec"] = [], [], []
        return run

    def pair(g, pr):
        def run():
            a2, row_t, q_t = ctx["a2"], ctx["row_t"], ctx["q_t"]
            h0 = g * hpg + 2 * pr
            c0 = h0 * SSD_HEAD_DIM
            xpair = u.xc[:, c0:c0 + LANES]
            rhs = jnp.concatenate([jnp.where(lane_lo, xpair, 0.0),
                                   jnp.where(lane_lo, 0.0, xpair)], axis=0).astype(BF16)
            tops, bots, pbs = [], [], []
            for h in (h0, h0 + 1):
                col = jnp.broadcast_to(a2[:, h:h + 1], (CHUNK, CHUNK))
                lmat = jnp.exp2(jnp.where(ctx["causal"], col - row_t[h:h + 1, :], NEG))
                tops.append((ctx["cb"] * lmat).astype(BF16))
                bots.append((ctx["bm_t"] * q_t[h:h + 1, :]).astype(BF16))
                pbs.append(jnp.exp2(col))
            lhs = jnp.concatenate([jnp.concatenate(tops, axis=1), jnp.concatenate(bots, axis=1)], axis=0)
            r = _dot(lhs, rhs)
            pp = jnp.where(lane_lo, pbs[0], pbs[1])
            ctx["y"].append(r[0:CHUNK, :] + ctx["y_off"][:, pr * LANES:(pr + 1) * LANES] * pp
                            + xpair * w.dskip[:, c0:c0 + LANES])
            ctx["st"].append(r[CHUNK:2 * CHUNK, :])
            ctx["dec"].append(pp[CHUNK - 1:CHUNK, :])
        return run

    def group_tail(g):
        def run():
            state_ref[g] = (state_ref[g] * jnp.concatenate(ctx["dec"], axis=1)
                            + jnp.concatenate(ctx["st"], axis=1))
            yg = jnp.concatenate(ctx["y"], axis=1) * _silu(u.z[:, g * gw:(g + 1) * gw])
            ms = jnp.mean(yg * yg, axis=-1, keepdims=True)
            u.y[:, g * gw:(g + 1) * gw] = (
                yg * lax.rsqrt(ms + EPS) * w.ssdnw[:, g * gw:(g + 1) * gw]).astype(BF16)
        return run

    for g in range(SSD_GROUPS):
        tasks.append((COST_SSD_PREP, group_head(g)))
        for pr in range(hpg // 2):
            tasks.append((COST_SSD_PAIR, pair(g, pr)))
        tasks.append((COST_SSD_GATE, group_tail(g)))
    return tasks


def _conformer_tasks(u, w):
    ctx = {}

    def conv(rt, t):
        def run():
            base = rt * CONF_TILE
            cols = slice(t * LANES, (t + 1) * LANES)
            acc = _rep_rows(w.cdb[:, cols], CONF_TILE)
            for k in range(CONF_KERNEL):
                r0 = base + GLU_PAD - (CONF_KERNEL - 1) + k
                acc = acc + _rep_rows(w.cdw[k, :, cols], CONF_TILE) * u.glu[t, r0:r0 + CONF_TILE, :]
            ctx.setdefault(rt, []).append(acc)
        return run

    def norm(rt):
        def run():
            base = rt * CONF_TILE
            acc = jnp.concatenate(ctx.pop(rt), axis=1)
            mu = jnp.mean(acc, axis=-1, keepdims=True)
            xc = acc - mu
            var = jnp.mean(xc * xc, axis=-1, keepdims=True)
            yl = xc * lax.rsqrt(var + EPS) * w.clnw[...] + w.clnb[...]
            u.y[base:base + CONF_TILE, SSD_INNER:SSD_INNER + CONF_WIDTH] = (
                _silu(yl) * _silu(u.cg[base:base + CONF_TILE, :])).astype(BF16)
        return run

    tasks = []
    for rt in range(CHUNK // CONF_TILE):
        tasks += [(COST_CONF_CONV, conv(rt, t)) for t in range(CONF_WIDTH // LANES)]
        tasks.append((COST_CONF_NORM, norm(rt)))
    return tasks


def _attention_tasks(u, w, bias_ref, first_pen):
    lane_lo = lax.broadcasted_iota(jnp.int32, (CHUNK, LANES), 1) < (LANES // 2)
    grp = ATTN_HEADS // ATTN_KV_HEADS

    ctx = {}

    def scores(kh):
        def run():
            kb = u.k[:, kh * LANES:(kh + 1) * LANES]
            qs = []
            for pr in range(grp // 2):
                c0 = (kh * grp + 2 * pr) * ATTN_HEAD_DIM
                qpair = u.q[:, c0:c0 + LANES].astype(F32)
                qs.append(jnp.where(lane_lo, qpair, 0.0).astype(BF16))
                qs.append(jnp.where(lane_lo, 0.0, qpair).astype(BF16))
            ctx["s"] = _dot_nt(jnp.concatenate(qs, axis=0), kb)
            ctx["e"], ctx["inv"] = [], []
        return run

    def softmax(kh, gi):
        def run():
            h = kh * grp + gi
            sh = ctx["s"][gi * CHUNK:(gi + 1) * CHUNK, :] + bias_ref[h * CHUNK:(h + 1) * CHUNK, :]
            if first_pen is not None:
                prev_half = lax.broadcasted_iota(jnp.int32, (CHUNK, 2 * CHUNK), 1) < CHUNK
                sh = sh + jnp.where(prev_half, first_pen, 0.0)
            sink = w.sinks[h]
            m = jnp.maximum(jnp.max(sh, axis=-1, keepdims=True), sink)
            e = jnp.exp(sh - m)
            denom = jnp.sum(e, axis=-1, keepdims=True) + jnp.exp(sink - m)
            ctx["e"].append(e.astype(BF16))
            ctx["inv"].append(1.0 / denom)
        return run

    def values(kh):
        def run():
            vb = u.v[:, kh * LANES:(kh + 1) * LANES]
            o = _dot(jnp.concatenate(ctx["e"], axis=0), vb)
            inv = ctx["inv"]
            outs = []
            for pr in range(grp // 2):
                o0 = o[(2 * pr) * CHUNK:(2 * pr + 1) * CHUNK, :] * inv[2 * pr]
                o1 = o[(2 * pr + 1) * CHUNK:(2 * pr + 2) * CHUNK, :] * inv[2 * pr + 1]
                outs.append(jnp.where(lane_lo, o0, o1))
            c0 = kh * grp * ATTN_HEAD_DIM
            gate = _silu(u.ag[:, c0:c0 + grp * ATTN_HEAD_DIM])
            u.y[:, SSD_INNER + CONF_WIDTH + c0:SSD_INNER + CONF_WIDTH + c0 + grp * ATTN_HEAD_DIM] = (
                jnp.concatenate(outs, axis=1) * gate).astype(BF16)
        return run

    tasks = []
    for kh in range(ATTN_KV_HEADS):
        tasks.append((COST_ATTN_QK, scores(kh)))
        tasks += [(COST_ATTN_SOFTMAX, softmax(kh, gi)) for gi in range(grp)]
        tasks.append((COST_ATTN_PV, values(kh)))
    return tasks


def _out_proj_tasks(u, w, get_x, get_p, out_ref, idx):
    def project():
        ov = _dot(u.y[...], w.wout[:, 0:D_MODEL])
        ms = jnp.mean(ov * ov, axis=-1, keepdims=True)
        out_ref[idx] = get_x() + ov * lax.rsqrt(ms + EPS) * w.postw[...]

    def embed():
        hv = out_ref[idx]
        gate = _sigmoid(_dot(hv.astype(BF16), w.wpg[:, 0:D_MODEL]))
        out_ref[idx] = hv + _dot(get_p().astype(BF16), w.wple[:, 0:D_MODEL]) * gate

    return [(MIX_WIDTH, project), (D_MODEL + PLE_DIM, embed)]


def _mixer_tasks(u, o, w, state_ref, bias_ref, first_pen):
    return ([(COST_SSD_PREP, lambda: _hand_over_history(u, o))] + _ssd_tasks(u, w, state_ref)
            + _conformer_tasks(u, w) + _attention_tasks(u, w, bias_ref, first_pen))


def _run_interleaved(main, side):
    main_total = sum(c for c, _ in main)
    side_total = sum(c for c, _ in side)
    done, side_cost, main_cost = 0, 0, 0
    for cost, task in main:
        task()
        main_cost += cost
        while done < len(side) and (side_cost + side[done][0] / 2) * main_total <= main_cost * side_total:
            side_cost += side[done][0]
            side[done][1]()
            done += 1
    for _, task in side[done:]:
        task()


class _LayerSinks:
    def __init__(self, ref, layer):
        self.ref, self.layer = ref, layer

    def __getitem__(self, h):
        return self.ref[self.layer, h]


def _layer_kernel(layer, steps_per_seq, *refs):
    (x_ref, xn_ref, p_ref, win_a, win_b, win_c, wout, wpg, wple, prew, convw, convb, dtb, alog, dskip,
     ssdnw, cdw, cdb, clnw, clnb, postw, bucket_ref, sinks, relb) = refs[:_N_IN]
    o_ref = refs[_N_IN]
    scratch = refs[_N_IN + 1:]
    n = len(_SET)
    ua = _Refs(**{name: r for (name, _, _), r in zip(_SET, scratch[:n])})
    ub = _Refs(**{name: r for (name, _, _), r in zip(_SET, scratch[n:2 * n])})
    state_ref, bias_ref = scratch[2 * n:]
    w = _Refs(win=dict(a=win_a, b=win_b, c=win_c), wout=wout, wpg=wpg, wple=wple, prew=prew, convw=convw,
              convb=convb, dtb=dtb, alog=alog, dskip=dskip, ssdnw=ssdnw, cdw=cdw, cdb=cdb, clnw=clnw,
              clnb=clnb, postw=postw, sinks=_LayerSinks(sinks, layer))

    step = pl.program_id(0)
    seq_start = lax.rem(step, steps_per_seq) == 0

    @pl.when(step == 0)
    def _():
        bucket = bucket_ref[...]
        qi = lax.broadcasted_iota(jnp.int32, (CHUNK, 2 * CHUNK), 0)
        si = lax.broadcasted_iota(jnp.int32, (CHUNK, 2 * CHUNK), 1)
        dist = qi + CHUNK - si
        in_window = (dist >= 0) & (dist < CHUNK)
        for h in range(ATTN_HEADS):
            acc = jnp.zeros((CHUNK, 2 * CHUNK), F32)
            for bkt in range(N_BUCKETS):
                acc = jnp.where(bucket == bkt, relb[bkt, h], acc)
            bias_ref[h * CHUNK:(h + 1) * CHUNK, :] = jnp.where(in_window, acc, NEG)
        for _, task in _in_proj_tasks(lambda: x_ref[0], ua, w):
            task()

    @pl.when(seq_start)
    def _():
        ua.xbc[:, 0:XBC_PAD, :] = jnp.zeros((SSD_CONV_DIM // LANES, XBC_PAD, LANES), F32)
        ua.glu[:, 0:GLU_PAD, :] = jnp.zeros((CONF_WIDTH // LANES, GLU_PAD, LANES), F32)
        ua.k[0:CHUNK, :] = jnp.zeros((CHUNK, 2 * LANES), BF16)
        ua.v[0:CHUNK, :] = jnp.zeros((CHUNK, 2 * LANES), BF16)
        state_ref[...] = jnp.zeros(state_ref.shape, F32)

    first_pen = jnp.where(seq_start, NEG, 0.0)

    _run_interleaved(_mixer_tasks(ua, ub, w, state_ref, bias_ref, first_pen),
                     _in_proj_tasks(lambda: x_ref[1], ub, w))
    _run_interleaved(_mixer_tasks(ub, ua, w, state_ref, bias_ref, None),
                     _out_proj_tasks(ua, w, lambda: x_ref[0], lambda: p_ref[0], o_ref, 0)
                     + _in_proj_tasks(lambda: xn_ref[...], ua, w))
    for _, task in _out_proj_tasks(ub, w, lambda: x_ref[1], lambda: p_ref[1], o_ref, 1):
        task()


def _split_w_in(w_in):
    piece_a = w_in[:, :, :A_COLS].astype(BF16)
    piece_b = w_in[:, :, B_START:].astype(BF16)
    k0 = B_START + 3 * CONF_WIDTH + ATTN_WIDTH

    def dup(c0):
        heads = [w_in[:, :, c0 + i * ATTN_HEAD_DIM:c0 + (i + 1) * ATTN_HEAD_DIM] for i in range(ATTN_KV_HEADS)]
        return [t for hd in heads for t in (hd, hd)]

    dt_pad = jnp.pad(w_in[:, :, A_COLS:B_START], ((0, 0), (0, 0), (0, LANES - SSD_HEADS)))
    piece_c = jnp.concatenate(dup(k0) + dup(k0 + KV_WIDTH) + [dt_pad], axis=2).astype(BF16)
    return piece_a, piece_b, piece_c


def _pad_pitch(wm):
    return jnp.pad(wm.astype(BF16), ((0, 0), (0, 0), (0, LANES)))


def _col(v):
    return v.astype(F32)[:, :, None]


def _row(v):
    return v.astype(F32)[:, None, :]


def _sublane_rep(v):
    v = v.astype(F32)
    return jnp.broadcast_to(v[..., None, :], v.shape[:-1] + (SUBLANES, v.shape[-1]))


def _layer_spec(arr, layer, **kw):
    rest = arr.shape[1:]
    return pl.BlockSpec((None,) + rest, lambda s: (layer,) + (0,) * len(rest), **kw)


def _layer(layer, xc, pc, steps_per_seq, weights, small, bucket, sinks, rel_bias):
    n_chunks = xc.shape[0]
    n_steps = n_chunks // CHUNKS_PER_STEP
    in_specs = (
        [pl.BlockSpec((CHUNKS_PER_STEP, CHUNK, D_MODEL), lambda s: (s, 0, 0)),
         pl.BlockSpec((None, CHUNK, D_MODEL),
                      lambda s: (jnp.minimum(CHUNKS_PER_STEP * (s + 1), n_chunks - 1), 0, 0)),
         pl.BlockSpec((None, CHUNKS_PER_STEP, CHUNK, PLE_DIM), lambda s: (layer, s, 0, 0))]
        + [_layer_spec(wt, layer, pipeline_mode=pl.Buffered(1)) for wt in weights]
        + [_layer_spec(sm, layer) for sm in small]
        + [pl.BlockSpec(bucket.shape, lambda s: (0, 0)),
           pl.BlockSpec(memory_space=pltpu.SMEM), pl.BlockSpec(memory_space=pltpu.SMEM)]
    )
    assert len(in_specs) == _N_IN
    scratch = ([pltpu.VMEM(shape, dtype) for _ in range(CHUNKS_PER_STEP) for (_, shape, dtype) in _SET]
               + [pltpu.VMEM((SSD_GROUPS, SSD_STATE, SSD_INNER // SSD_GROUPS), F32),
                  pltpu.VMEM((ATTN_HEADS * CHUNK, 2 * CHUNK), F32)])
    return pl.pallas_call(
        functools.partial(_layer_kernel, layer, steps_per_seq),
        out_shape=jax.ShapeDtypeStruct(xc.shape, xc.dtype),
        grid=(n_steps,),
        in_specs=in_specs,
        out_specs=pl.BlockSpec((CHUNKS_PER_STEP, CHUNK, D_MODEL), lambda s: (s, 0, 0)),
        scratch_shapes=scratch,
        compiler_params=pltpu.CompilerParams(
            dimension_semantics=("arbitrary",),
            vmem_limit_bytes=VMEM_LIMIT_BYTES),
        name="hybrid_layer",
    )(xc, xc, pc, *weights, *small, bucket, sinks, rel_bias)


def kernel(x, p, pre_norm_w, w_in, ssd_conv_w, ssd_conv_b, ssd_dt_bias, ssd_a_log, ssd_d, ssd_norm_w,
           conf_dw_w, conf_dw_b, conf_ln_w, conf_ln_b, attn_sinks, rel_bias, w_out, post_norm_w,
           ple_proj, ple_gate):
    bsz, seq, _ = x.shape
    depth = p.shape[0]
    step_rows = CHUNKS_PER_STEP * CHUNK
    assert seq % step_rows == 0
    n_chunks = bsz * seq // CHUNK
    weights = list(_split_w_in(w_in)) + [_pad_pitch(w_out), _pad_pitch(ple_gate), _pad_pitch(ple_proj)]
    small = [
        _row(pre_norm_w), _sublane_rep(ssd_conv_w), _sublane_rep(ssd_conv_b), _col(ssd_dt_bias),
        _col(ssd_a_log), _row(jnp.repeat(ssd_d, SSD_HEAD_DIM, axis=1)), _row(ssd_norm_w),
        _sublane_rep(conf_dw_w), _sublane_rep(conf_dw_b), _row(conf_ln_w), _row(conf_ln_b),
        _row(post_norm_w),
    ]
    bucket = jnp.asarray(_t5_bucket_table())
    pc = p.reshape(depth, n_chunks, CHUNK, PLE_DIM)
    h = x.reshape(n_chunks, CHUNK, D_MODEL)
    for layer in range(depth):
        h = _layer(layer, h, pc, seq // step_rows, weights, small, bucket,
                   attn_sinks.astype(F32), rel_bias.astype(F32))
    return h.reshape(x.shape)


def _t5_bucket_table():
    q = np.arange(CHUNK)[:, None]
    s = np.arange(2 * CHUNK)[None, :]
    d = np.maximum(q + CHUNK - s, 0)
    max_exact = N_BUCKETS // 2
    large = max_exact + (np.log(np.maximum(d, 1).astype(np.float32) / max_exact)
                         / math.log(MAX_DISTANCE / max_exact) * (N_BUCKETS - max_exact)).astype(np.int32)
    large = np.minimum(large, N_BUCKETS - 1)
    return np.where(d < max_exact, d, large).astype(np.int32)
```

```python
import functools
import math

import numpy as np
import jax
import jax.numpy as jnp
from jax import lax
from jax.experimental import pallas as pl
from jax.experimental.pallas import tpu as pltpu

F32 = jnp.float32
BF16 = jnp.bfloat16

D_MODEL = 1024
SSD_INNER = 1024
SSD_HEADS = 16
SSD_HEAD_DIM = 64
SSD_GROUPS = 2
SSD_STATE = 128
SSD_CONV = 4
CHUNK = 128
SSD_CONV_DIM = SSD_INNER + 2 * SSD_GROUPS * SSD_STATE
CONF_WIDTH = 512
CONF_KERNEL = 31
ATTN_HEADS = 8
ATTN_KV_HEADS = 2
ATTN_HEAD_DIM = 64
ATTN_WIDTH = ATTN_HEADS * ATTN_HEAD_DIM
N_BUCKETS = 32
MAX_DISTANCE = 128
PLE_DIM = 256
MIX_WIDTH = SSD_INNER + CONF_WIDTH + ATTN_WIDTH
EPS = 1e-6
LOG2E = math.log2(math.e)
NEG = -1e30

LANES = 128
SUBLANES = 8
VMEM_LIMIT_BYTES = 56 * 1024 * 1024

CHUNKS_PER_STEP = 4
XBC_PAD = SUBLANES
GLU_PAD = 32
CONF_TILE = 32
PROJ_SEG = 256

COST_SSD_CONV = 50
COST_SSD_PREP = 40
COST_SSD_PAIR = 100
COST_SSD_GATE = 280
COST_CONF_CONV = 70
COST_CONF_NORM = 60
COST_ATTN_QK = 30
COST_ATTN_SOFTMAX = 70
COST_ATTN_PV = 60

A_COLS = SSD_INNER + SSD_CONV_DIM
B_START = A_COLS + SSD_HEADS
KV_WIDTH = ATTN_KV_HEADS * ATTN_HEAD_DIM
B_COLS = 3 * CONF_WIDTH + 2 * ATTN_WIDTH + 2 * KV_WIDTH
Z0 = ("a", 0)
XBC0 = ("a", SSD_INNER)
CIN0 = ("b", 0)
CG0 = ("b", 2 * CONF_WIDTH)
Q0 = ("b", 3 * CONF_WIDTH)
AG0 = ("b", 3 * CONF_WIDTH + ATTN_WIDTH + 2 * KV_WIDTH)
K0 = ("c", 0)
V0 = ("c", 2 * LANES)
DT0 = ("c", 4 * LANES)
C_COLS = 5 * LANES

_SET = (
    ("hn", (CHUNK, D_MODEL), BF16),
    ("z", (CHUNK, SSD_INNER), F32),
    ("xbc", (SSD_CONV_DIM // LANES, XBC_PAD + CHUNK, LANES), F32),
    ("xc", (CHUNK, SSD_CONV_DIM), F32),
    ("dt", (CHUNK, LANES), F32),
    ("glu", (CONF_WIDTH // LANES, GLU_PAD + CHUNK, LANES), F32),
    ("cg", (CHUNK, CONF_WIDTH), F32),
    ("q", (CHUNK, ATTN_WIDTH), BF16),
    ("k", (2 * CHUNK, 2 * LANES), BF16),
    ("v", (2 * CHUNK, 2 * LANES), BF16),
    ("ag", (CHUNK, ATTN_WIDTH), F32),
    ("y", (CHUNK, MIX_WIDTH), BF16),
)
_N_IN = 24


def _sigmoid(x):
    return 1.0 / (1.0 + jnp.exp(-x))


def _silu(x):
    half = 0.5 * x
    return half + half * jnp.tanh(half)


def _softplus(x):
    return jnp.maximum(x, 0.0) + jnp.log(1.0 + jnp.exp(-jnp.abs(x)))


def _split3(x):
    hi = x.astype(BF16)
    r1 = x - hi.astype(F32)
    mid = r1.astype(BF16)
    lo = (r1 - mid.astype(F32)).astype(BF16)
    return hi, mid, lo


def _dot(a, b):
    return jnp.dot(a, b, preferred_element_type=F32)


def _dot_nt(a, b):
    return lax.dot_general(a, b, (((1,), (1,)), ((), ())), preferred_element_type=F32)


class _Refs:
    def __init__(self, **kw):
        self.__dict__.update(kw)


def _in_proj_tasks(get_x, u, w):
    def prenorm():
        xv = get_x()
        ms = jnp.mean(xv * xv, axis=-1, keepdims=True)
        u.hn[...] = (xv * lax.rsqrt(ms + EPS) * w.prew[...]).astype(BF16)

    def proj(seg0, width, off=0):
        piece, c0 = seg0
        return _dot(u.hn[...], w.win[piece][:, c0 + off:c0 + off + width])

    def glu(half):
        def run():
            c0 = half * (CONF_WIDTH // 2)
            av = proj(CIN0, CONF_WIDTH // 2, c0)
            gv = proj(CIN0, CONF_WIDTH // 2, CONF_WIDTH + c0)
            store_slabs(u.glu, GLU_PAD, c0)(av * _sigmoid(gv))
        return (CONF_WIDTH, run)

    def seg(seg0, width, store, off=0):
        return (width, lambda: store(proj(seg0, width, off)))

    def store_slabs(ref, r0, c0):
        def store(val):
            for t in range(val.shape[1] // LANES):
                ref[c0 // LANES + t, r0:r0 + CHUNK, :] = val[:, t * LANES:(t + 1) * LANES]
        return store

    def store_to(ref, r0, c0, width, cast=None, scale=None):
        def store(val):
            if scale is not None:
                val = val * scale
            ref[r0:r0 + CHUNK, c0:c0 + width] = val if cast is None else val.astype(cast)
        return store

    seg_w = PROJ_SEG
    tasks = [(LANES, prenorm), glu(0), glu(1)]
    for c0 in range(0, CONF_WIDTH, seg_w):
        tasks.append(seg(CG0, seg_w, store_to(u.cg, 0, c0, seg_w), c0))
    for c0 in range(0, SSD_CONV_DIM, seg_w):
        tasks.append(seg(XBC0, seg_w, store_slabs(u.xbc, XBC_PAD, c0), c0))
    tasks.append(seg(DT0, LANES, store_to(u.dt, 0, 0, LANES)))
    for c0 in range(0, SSD_INNER, seg_w):
        tasks.append(seg(Z0, seg_w, store_to(u.z, 0, c0, seg_w), c0))
    for c0 in range(0, ATTN_WIDTH, seg_w):
        tasks.append(seg(Q0, seg_w, store_to(u.q, 0, c0, seg_w, cast=BF16, scale=ATTN_HEAD_DIM ** -0.5), c0))
        tasks.append(seg(AG0, seg_w, store_to(u.ag, 0, c0, seg_w), c0))
    tasks.append(seg(K0, 2 * LANES, store_to(u.k, CHUNK, 0, 2 * LANES, cast=BF16)))
    tasks.append(seg(V0, 2 * LANES, store_to(u.v, CHUNK, 0, 2 * LANES, cast=BF16)))
    return tasks


def _hand_over_history(u, o):
    o.xbc[:, 0:XBC_PAD, :] = u.xbc[:, CHUNK:CHUNK + XBC_PAD, :]
    o.glu[:, 0:GLU_PAD, :] = u.glu[:, CHUNK:CHUNK + GLU_PAD, :]
    o.k[0:CHUNK, :] = u.k[CHUNK:2 * CHUNK, :]
    o.v[0:CHUNK, :] = u.v[CHUNK:2 * CHUNK, :]


def _rep_rows(v, rows):
    return jnp.concatenate([v] * (rows // SUBLANES), axis=0)


def _ssd_tasks(u, w, state_ref):
    tasks = []
    ctx = {}
    lane_lo = lax.broadcasted_iota(jnp.int32, (CHUNK, LANES), 1) < (LANES // 2)
    hpg = SSD_HEADS // SSD_GROUPS
    gw = SSD_INNER // SSD_GROUPS

    def conv(t):
        def run():
            cols = slice(t * LANES, (t + 1) * LANES)
            acc = _rep_rows(w.convb[:, cols], CHUNK)
            for k in range(SSD_CONV):
                r0 = XBC_PAD - (SSD_CONV - 1) + k
                acc = acc + _rep_rows(w.convw[k, :, cols], CHUNK) * u.xbc[t, r0:r0 + CHUNK, :]
            u.xc[:, cols] = _silu(acc)
        return run
    for t in range(SSD_CONV_DIM // LANES):
        tasks.append((COST_SSD_CONV, conv(t)))

    def prep():
        li = lax.broadcasted_iota(jnp.int32, (CHUNK, CHUNK), 0)
        si = lax.broadcasted_iota(jnp.int32, (CHUNK, CHUNK), 1)
        causal = li >= si
        triu = (li <= si).astype(F32).astype(BF16)
        dt_t = _softplus(u.dt[...].T[0:SSD_HEADS, :] + w.dtb[...])
        a = -jnp.exp(w.alog[...])
        hi, mid, lo = _split3(dt_t * a)
        a_cs_t = _dot(hi, triu) + _dot(mid, triu) + _dot(lo, triu)
        a2_t = a_cs_t * LOG2E
        a2 = jnp.concatenate([a2_t, jnp.zeros((CHUNK - SSD_HEADS, CHUNK), F32)], axis=0).T
        q_t = dt_t * jnp.exp2(a2_t[:, CHUNK - 1:CHUNK] - a2_t)
        row_t = a2_t - jnp.log2(dt_t)
        ctx.update(causal=causal, a2=a2, row_t=row_t, q_t=q_t)
    tasks.append((COST_SSD_PREP, prep))

    def group_head(g):
        def run():
            bm = u.xc[:, SSD_INNER + g * SSD_STATE:SSD_INNER + (g + 1) * SSD_STATE]
            cm = u.xc[:, SSD_INNER + (SSD_GROUPS + g) * SSD_STATE:SSD_INNER + (SSD_GROUPS + g + 1) * SSD_STATE]
            cmb = cm.astype(BF16)
            ctx["cb"] = _dot_nt(cmb, bm.astype(BF16))
            ctx["bm_t"] = bm.T
            ctx["y_off"] = _dot(cmb, state_ref[g].astype(BF16))
            ctx["y"], ctx["st"], ctx["dec"] = [], [], []
        return run

    def pair(g, pr):
        def run():
            a2, row_t, q_t = ctx["a2"], ctx["row_t"], ctx["q_t"]
            h0 = g * hpg + 2 * pr
            c0 = h0 * SSD_HEAD_DIM
            xpair = u.xc[:, c0:c0 + LANES]
            rhs = jnp.concatenate([jnp.where(lane_lo, xpair, 0.0),
                                   jnp.where(lane_lo, 0.0, xpair)], axis=0).astype(BF16)
            tops, bots, pbs = [], [], []
            for h in (h0, h0 + 1):
                col = jnp.broadcast_to(a2[:, h:h + 1], (CHUNK, CHUNK))
                lmat = jnp.exp2(jnp.where(ctx["causal"], col - row_t[h:h + 1, :], NEG))
                tops.append((ctx["cb"] * lmat).astype(BF16))
                bots.append((ctx["bm_t"] * q_t[h:h + 1, :]).astype(BF16))
                pbs.append(jnp.exp2(col))
            lhs = jnp.concatenate([jnp.concatenate(tops, axis=1), jnp.concatenate(bots, axis=1)], axis=0)
            r = _dot(lhs, rhs)
            pp = jnp.where(lane_lo, pbs[0], pbs[1])
            ctx["y"].append(r[0:CHUNK, :] + ctx["y_off"][:, pr * LANES:(pr + 1) * LANES] * pp
                            + xpair * w.dskip[:, c0:c0 + LANES])
            ctx["st"].append(r[CHUNK:2 * CHUNK, :])
            ctx["dec"].append(pp[CHUNK - 1:CHUNK, :])
        return run

    def group_tail(g):
        def run():
            state_ref[g] = (state_ref[g] * jnp.concatenate(ctx["dec"], axis=1)
                            + jnp.concatenate(ctx["st"], axis=1))
            yg = jnp.concatenate(ctx["y"], axis=1) * _silu(u.z[:, g * gw:(g + 1) * gw])
            ms = jnp.mean(yg * yg, axis=-1, keepdims=True)
            u.y[:, g * gw:(g + 1) * gw] = (
                yg * lax.rsqrt(ms + EPS) * w.ssdnw[:, g * gw:(g + 1) * gw]).astype(BF16)
        return run

    for g in range(SSD_GROUPS):
        tasks.append((COST_SSD_PREP, group_head(g)))
        for pr in range(hpg // 2):
            tasks.append((COST_SSD_PAIR, pair(g, pr)))
        tasks.append((COST_SSD_GATE, group_tail(g)))
    return tasks


def _conformer_tasks(u, w):
    ctx = {}

    def conv(rt, t):
        def run():
            base = rt * CONF_TILE
            cols = slice(t * LANES, (t + 1) * LANES)
            acc = _rep_rows(w.cdb[:, cols], CONF_TILE)
            for k in range(CONF_KERNEL):
                r0 = base + GLU_PAD - (CONF_KERNEL - 1) + k
                acc = acc + _rep_rows(w.cdw[k, :, cols], CONF_TILE) * u.glu[t, r0:r0 + CONF_TILE, :]
            ctx.setdefault(rt, []).append(acc)
        return run

    def norm(rt):
        def run():
            base = rt * CONF_TILE
            acc = jnp.concatenate(ctx.pop(rt), axis=1)
            mu = jnp.mean(acc, axis=-1, keepdims=True)
            xc = acc - mu
            var = jnp.mean(xc * xc, axis=-1, keepdims=True)
            yl = xc * lax.rsqrt(var + EPS) * w.clnw[...] + w.clnb[...]
            u.y[base:base + CONF_TILE, SSD_INNER:SSD_INNER + CONF_WIDTH] = (
                _silu(yl) * _silu(u.cg[base:base + CONF_TILE, :])).astype(BF16)
        return run

    tasks = []
    for rt in range(CHUNK // CONF_TILE):
        tasks += [(COST_CONF_CONV, conv(rt, t)) for t in range(CONF_WIDTH // LANES)]
        tasks.append((COST_CONF_NORM, norm(rt)))
    return tasks


def _attention_tasks(u, w, bias_ref, first_pen):
    lane_lo = lax.broadcasted_iota(jnp.int32, (CHUNK, LANES), 1) < (LANES // 2)
    grp = ATTN_HEADS // ATTN_KV_HEADS

    ctx = {}

    def scores(kh):
        def run():
            kb = u.k[:, kh * LANES:(kh + 1) * LANES]
            qs = []
            for pr in range(grp // 2):
                c0 = (kh * grp + 2 * pr) * ATTN_HEAD_DIM
                qpair = u.q[:, c0:c0 + LANES].astype(F32)
                qs.append(jnp.where(lane_lo, qpair, 0.0).astype(BF16))
                qs.append(jnp.where(lane_lo, 0.0, qpair).astype(BF16))
            ctx["s"] = _dot_nt(jnp.concatenate(qs, axis=0), kb)
            ctx["e"], ctx["inv"] = [], []
        return run

    def softmax(kh, gi):
        def run():
            h = kh * grp + gi
            sh = ctx["s"][gi * CHUNK:(gi + 1) * CHUNK, :] + bias_ref[h * CHUNK:(h + 1) * CHUNK, :]
            if first_pen is not None:
                prev_half = lax.broadcasted_iota(jnp.int32, (CHUNK, 2 * CHUNK), 1) < CHUNK
                sh = sh + jnp.where(prev_half, first_pen, 0.0)
            sink = w.sinks[h]
            m = jnp.maximum(jnp.max(sh, axis=-1, keepdims=True), sink)
            e = jnp.exp(sh - m)
            denom = jnp.sum(e, axis=-1, keepdims=True) + jnp.exp(sink - m)
            ctx["e"].append(e.astype(BF16))
            ctx["inv"].append(1.0 / denom)
        return run

    def values(kh):
        def run():
            vb = u.v[:, kh * LANES:(kh + 1) * LANES]
            o = _dot(jnp.concatenate(ctx["e"], axis=0), vb)
            inv = ctx["inv"]
            outs = []
            for pr in range(grp // 2):
                o0 = o[(2 * pr) * CHUNK:(2 * pr + 1) * CHUNK, :] * inv[2 * pr]
                o1 = o[(2 * pr + 1) * CHUNK:(2 * pr + 2) * CHUNK, :] * inv[2 * pr + 1]
                outs.append(jnp.where(lane_lo, o0, o1))
            c0 = kh * grp * ATTN_HEAD_DIM
            gate = _silu(u.ag[:, c0:c0 + grp * ATTN_HEAD_DIM])
            u.y[:, SSD_INNER + CONF_WIDTH + c0:SSD_INNER + CONF_WIDTH + c0 + grp * ATTN_HEAD_DIM] = (
                jnp.concatenate(outs, axis=1) * gate).astype(BF16)
        return run

    tasks = []
    for kh in range(ATTN_KV_HEADS):
        tasks.append((COST_ATTN_QK, scores(kh)))
        tasks += [(COST_ATTN_SOFTMAX, softmax(kh, gi)) for gi in range(grp)]
        tasks.append((COST_ATTN_PV, values(kh)))
    return tasks


def _out_proj_tasks(u, w, get_x, get_p, out_ref, idx):
    def project():
        ov = _dot(u.y[...], w.wout[:, 0:D_MODEL])
        ms = jnp.mean(ov * ov, axis=-1, keepdims=True)
        out_ref[idx] = get_x() + ov * lax.rsqrt(ms + EPS) * w.postw[...]

    def embed():
        hv = out_ref[idx]
        gate = _sigmoid(_dot(hv.astype(BF16), w.wpg[:, 0:D_MODEL]))
        out_ref[idx] = hv + _dot(get_p().astype(BF16), w.wple[:, 0:D_MODEL]) * gate

    return [(MIX_WIDTH, project), (D_MODEL + PLE_DIM, embed)]


def _mixer_tasks(u, o, w, state_ref, bias_ref, first_pen):
    return ([(COST_SSD_PREP, lambda: _hand_over_history(u, o))] + _ssd_tasks(u, w, state_ref)
            + _conformer_tasks(u, w) + _attention_tasks(u, w, bias_ref, first_pen))


def _run_interleaved(main, side):
    main_total = sum(c for c, _ in main)
    side_total = sum(c for c, _ in side)
    done, side_cost, main_cost = 0, 0, 0
    for cost, task in main:
        task()
        main_cost += cost
        while done < len(side) and (side_cost + side[done][0] / 2) * main_total <= main_cost * side_total:
            side_cost += side[done][0]
            side[done][1]()
            done += 1
    for _, task in side[done:]:
        task()


class _LayerSinks:
    def __init__(self, ref, layer):
        self.ref, self.layer = ref, layer

    def __getitem__(self, h):
        return self.ref[self.layer, h]


def _layer_kernel(layer, steps_per_seq, *refs):
    (x_ref, xn_ref, p_ref, win_a, win_b, win_c, wout, wpg, wple, prew, convw, convb, dtb, alog, dskip,
     ssdnw, cdw, cdb, clnw, clnb, postw, bucket_ref, sinks, relb) = refs[:_N_IN]
    o_ref = refs[_N_IN]
    scratch = refs[_N_IN + 1:]
    n = len(_SET)
    state_ref, bias_ref = scratch[CHUNKS_PER_STEP * n:]
    w = _Refs(win=dict(a=win_a, b=win_b, c=win_c), wout=wout, wpg=wpg, wple=wple, prew=prew, convw=convw,
              convb=convb, dtb=dtb, alog=alog, dskip=dskip, ssdnw=ssdnw, cdw=cdw, cdb=cdb, clnw=clnw,
              clnb=clnb, postw=postw, sinks=_LayerSinks(sinks, layer))

    step = pl.program_id(0)
    seq_start = lax.rem(step, steps_per_seq) == 0
    sets = [_Refs(**{name: r for (name, _, _), r in zip(_SET, scratch[c * n:(c + 1) * n])})
            for c in range(CHUNKS_PER_STEP)]
    ua = sets[0]

    @pl.when(step == 0)
    def _():
        bucket = bucket_ref[...]
        qi = lax.broadcasted_iota(jnp.int32, (CHUNK, 2 * CHUNK), 0)
        si = lax.broadcasted_iota(jnp.int32, (CHUNK, 2 * CHUNK), 1)
        dist = qi + CHUNK - si
        in_window = (dist >= 0) & (dist < CHUNK)
        for h in range(ATTN_HEADS):
            acc = jnp.zeros((CHUNK, 2 * CHUNK), F32)
            for bkt in range(N_BUCKETS):
                acc = jnp.where(bucket == bkt, relb[bkt, h], acc)
            bias_ref[h * CHUNK:(h + 1) * CHUNK, :] = jnp.where(in_window, acc, NEG)
        for _, task in _in_proj_tasks(lambda: x_ref[0], ua, w):
            task()

    @pl.when(seq_start)
    def _():
        ua.xbc[:, 0:XBC_PAD, :] = jnp.zeros((SSD_CONV_DIM // LANES, XBC_PAD, LANES), F32)
        ua.glu[:, 0:GLU_PAD, :] = jnp.zeros((CONF_WIDTH // LANES, GLU_PAD, LANES), F32)
        ua.k[0:CHUNK, :] = jnp.zeros((CHUNK, 2 * LANES), BF16)
        ua.v[0:CHUNK, :] = jnp.zeros((CHUNK, 2 * LANES), BF16)
        state_ref[...] = jnp.zeros(state_ref.shape, F32)

    first_pen = jnp.where(seq_start, NEG, 0.0)

    last = CHUNKS_PER_STEP - 1
    for c, u in enumerate(sets):
        nxt = sets[(c + 1) % CHUNKS_PER_STEP]
        around = []
        if c > 0:
            around += _out_proj_tasks(sets[c - 1], w, lambda c=c: x_ref[c - 1], lambda c=c: p_ref[c - 1],
                                      o_ref, c - 1)
        around += _in_proj_tasks((lambda c=c: x_ref[c + 1]) if c < last else (lambda: xn_ref[...]), nxt, w)
        _run_interleaved(_mixer_tasks(u, nxt, w, state_ref, bias_ref, first_pen if c == 0 else None), around)
    for _, task in _out_proj_tasks(sets[last], w, lambda: x_ref[last], lambda: p_ref[last], o_ref, last):
        task()


def _split_w_in(w_in):
    piece_a = w_in[:, :, :A_COLS].astype(BF16)
    piece_b = w_in[:, :, B_START:].astype(BF16)
    k0 = B_START + 3 * CONF_WIDTH + ATTN_WIDTH

    def dup(c0):
        heads = [w_in[:, :, c0 + i * ATTN_HEAD_DIM:c0 + (i + 1) * ATTN_HEAD_DIM] for i in range(ATTN_KV_HEADS)]
        return [t for hd in heads for t in (hd, hd)]

    dt_pad = jnp.pad(w_in[:, :, A_COLS:B_START], ((0, 0), (0, 0), (0, LANES - SSD_HEADS)))
    piece_c = jnp.concatenate(dup(k0) + dup(k0 + KV_WIDTH) + [dt_pad], axis=2).astype(BF16)
    return piece_a, piece_b, piece_c


def _pad_pitch(wm):
    return jnp.pad(wm.astype(BF16), ((0, 0), (0, 0), (0, LANES)))


def _col(v):
    return v.astype(F32)[:, :, None]


def _row(v):
    return v.astype(F32)[:, None, :]


def _sublane_rep(v):
    v = v.astype(F32)
    return jnp.broadcast_to(v[..., None, :], v.shape[:-1] + (SUBLANES, v.shape[-1]))


def _layer_spec(arr, layer, **kw):
    rest = arr.shape[1:]
    return pl.BlockSpec((None,) + rest, lambda s: (layer,) + (0,) * len(rest), **kw)


def _layer(layer, xc, pc, steps_per_seq, weights, small, bucket, sinks, rel_bias):
    n_chunks = xc.shape[0]
    n_steps = n_chunks // CHUNKS_PER_STEP
    in_specs = (
        [pl.BlockSpec((CHUNKS_PER_STEP, CHUNK, D_MODEL), lambda s: (s, 0, 0)),
         pl.BlockSpec((None, CHUNK, D_MODEL),
                      lambda s: (jnp.minimum(CHUNKS_PER_STEP * (s + 1), n_chunks - 1), 0, 0)),
         pl.BlockSpec((None, CHUNKS_PER_STEP, CHUNK, PLE_DIM), lambda s: (layer, s, 0, 0))]
        + [_layer_spec(wt, layer, pipeline_mode=pl.Buffered(1)) for wt in weights]
        + [_layer_spec(sm, layer) for sm in small]
        + [pl.BlockSpec(bucket.shape, lambda s: (0, 0)),
           pl.BlockSpec(memory_space=pltpu.SMEM), pl.BlockSpec(memory_space=pltpu.SMEM)]
    )
    assert len(in_specs) == _N_IN
    scratch = ([pltpu.VMEM(shape, dtype) for _ in range(CHUNKS_PER_STEP) for (_, shape, dtype) in _SET]
               + [pltpu.VMEM((SSD_GROUPS, SSD_STATE, SSD_INNER // SSD_GROUPS), F32),
                  pltpu.VMEM((ATTN_HEADS * CHUNK, 2 * CHUNK), F32)])
    return pl.pallas_call(
        functools.partial(_layer_kernel, layer, steps_per_seq),
        out_shape=jax.ShapeDtypeStruct(xc.shape, xc.dtype),
        grid=(n_steps,),
        in_specs=in_specs,
        out_specs=pl.BlockSpec((CHUNKS_PER_STEP, CHUNK, D_MODEL), lambda s: (s, 0, 0)),
        scratch_shapes=scratch,
        compiler_params=pltpu.CompilerParams(
            dimension_semantics=("arbitrary",),
            vmem_limit_bytes=VMEM_LIMIT_BYTES),
        name="hybrid_layer",
    )(xc, xc, pc, *weights, *small, bucket, sinks, rel_bias)


def kernel(x, p, pre_norm_w, w_in, ssd_conv_w, ssd_conv_b, ssd_dt_bias, ssd_a_log, ssd_d, ssd_norm_w,
           conf_dw_w, conf_dw_b, conf_ln_w, conf_ln_b, attn_sinks, rel_bias, w_out, post_norm_w,
           ple_proj, ple_gate):
    bsz, seq, _ = x.shape
    depth = p.shape[0]
    step_rows = CHUNKS_PER_STEP * CHUNK
    assert seq % step_rows == 0
    n_chunks = bsz * seq // CHUNK
    weights = list(_split_w_in(w_in)) + [_pad_pitch(w_out), _pad_pitch(ple_gate), _pad_pitch(ple_proj)]
    small = [
        _row(pre_norm_w), _sublane_rep(ssd_conv_w), _sublane_rep(ssd_conv_b), _col(ssd_dt_bias),
        _col(ssd_a_log), _row(jnp.repeat(ssd_d, SSD_HEAD_DIM, axis=1)), _row(ssd_norm_w),
        _sublane_rep(conf_dw_w), _sublane_rep(conf_dw_b), _row(conf_ln_w), _row(conf_ln_b),
        _row(post_norm_w),
    ]
    bucket = jnp.asarray(_t5_bucket_table())
    pc = p.reshape(depth, n_chunks, CHUNK, PLE_DIM)
    h = x.reshape(n_chunks, CHUNK, D_MODEL)
    for layer in range(depth):
        h = _layer(layer, h, pc, seq // step_rows, weights, small, bucket,
                   attn_sinks.astype(F32), rel_bias.astype(F32))
    return h.reshape(x.shape)


def _t5_bucket_table():
    q = np.arange(CHUNK)[:, None]
    s = np.arange(2 * CHUNK)[None, :]
    d = np.maximum(q + CHUNK - s, 0)
    max_exact = N_BUCKETS // 2
    large = max_exact + (np.log(np.maximum(d, 1).astype(np.float32) / max_exact)
                         / math.log(MAX_DISTANCE / max_exact) * (N_BUCKETS - max_exact)).astype(np.int32)
    large = np.minimum(large, N_BUCKETS - 1)
    return np.where(d < max_exact, d, large).astype(np.int32)
```
